```python
import math
import jax, jax.numpy as jnp
from jax import lax
import numpy as np

D_MODEL = 1024
BATCH = 1
SEQ = 16384
DEPTH = 1
DEC_BATCH = 32
DEC_SEQ = 2048
PAST_LEN = 128

HEAD_DIM = 64
N_ATTN_HEADS = 8
N_KV_HEADS = 2
GQA_GROUP = N_ATTN_HEADS // N_KV_HEADS
ATTN_WIDTH = N_ATTN_HEADS * HEAD_DIM
KV_WIDTH = N_KV_HEADS * HEAD_DIM
N_MLSTM_HEADS = 8
MLSTM_WIDTH = N_MLSTM_HEADS * HEAD_DIM
MIX_WIDTH = ATTN_WIDTH + MLSTM_WIDTH
WINDOW = 128
BLOCK = 128
N_BUCKETS = 32
MAX_DISTANCE = 128
CHUNK = 128
CONV_WIDTH = 5
D_FF = 2816
EPS = 1e-6
NEG = -1e30
SPLIT_SIZES = (ATTN_WIDTH, KV_WIDTH, KV_WIDTH,
               MLSTM_WIDTH, MLSTM_WIDTH, MLSTM_WIDTH,
               MLSTM_WIDTH, 4 * N_MLSTM_HEADS)
IN_WIDTH = sum(SPLIT_SIZES)

kernel_name = "hymba_swa_mlstm_macaron_encoder"


def rmsnorm(x, g):
    xf = x.astype(jnp.float32)
    y = xf * lax.rsqrt(jnp.mean(xf * xf, axis=-1, keepdims=True) + EPS)
    return (y * g.astype(jnp.float32)).astype(x.dtype)


def swiglu(x, w_gu, w_down):
    g, u = jnp.split(x @ w_gu, 2, axis=-1)
    return (jax.nn.silu(g) * u) @ w_down


def t5_bucket(rel):
    nb = N_BUCKETS // 2
    ret = (rel > 0).astype(np.int32) * nb
    n = np.abs(rel)
    max_exact = nb // 2
    large = max_exact + (np.log(np.maximum(n, 1) / max_exact)
                         / math.log(MAX_DISTANCE / max_exact) * (nb - max_exact)).astype(np.int32)
    large = np.minimum(large, nb - 1)
    return (ret + np.where(n < max_exact, n, large)).astype(np.int32)


def banded_attention(q, k, v, rel_table, sink):
    B, S = q.shape[:2]
    nb = S // BLOCK
    f32 = jnp.float32
    qb = q.astype(f32).reshape(B, nb, BLOCK, N_KV_HEADS, GQA_GROUP, HEAD_DIM)
    pad = ((0, 0), (BLOCK, BLOCK), (0, 0), (0, 0))
    kb = jnp.pad(k.astype(f32), pad).reshape(B, nb + 2, BLOCK, N_KV_HEADS, HEAD_DIM)
    vb = jnp.pad(v.astype(f32), pad).reshape(B, nb + 2, BLOCK, N_KV_HEADS, HEAD_DIM)
    kw = jnp.concatenate([kb[:, :-2], kb[:, 1:-1], kb[:, 2:]], axis=2)
    vw = jnp.concatenate([vb[:, :-2], vb[:, 1:-1], vb[:, 2:]], axis=2)
    s = jnp.einsum('bnqhgd,bnkhd->bnhgqk', qb, kw) * (HEAD_DIM ** -0.5)
    qi = np.arange(BLOCK)[:, None]
    kj = np.arange(3 * BLOCK)[None, :]
    rel = (kj - BLOCK) - qi
    key_pos = (np.arange(nb)[:, None, None] - 1) * BLOCK + kj[None]
    mask = (np.abs(rel) <= WINDOW)[None] & (key_pos >= 0) & (key_pos < S)
    bias = rel_table.astype(f32)[t5_bucket(rel)]
    bias = jnp.transpose(bias, (2, 0, 1)).reshape(N_KV_HEADS, GQA_GROUP, BLOCK, 3 * BLOCK)
    s = jnp.where(jnp.asarray(mask)[None, :, None, None], s + bias[None, None], NEG)
    sk = sink.astype(f32).reshape(1, 1, N_KV_HEADS, GQA_GROUP, 1, 1)
    m = jnp.maximum(s.max(axis=-1, keepdims=True), sk)
    p = jnp.exp(s - m)
    den = p.sum(axis=-1, keepdims=True) + jnp.exp(sk - m)
    o = jnp.einsum('bnhgqk,bnkhd->bnqhgd', p / den, vw)
    return o.reshape(B, S, ATTN_WIDTH)


def centred_dwconv(x, w):
    C = x.shape[-1]
    return lax.conv_general_dilated(x, w[:, None, :].astype(x.dtype), window_strides=(1,),
                                    padding=[(CONV_WIDTH // 2, CONV_WIDTH // 2)],
                                    dimension_numbers=('NWC', 'WIO', 'NWC'), feature_group_count=C)


def mlstm_chunkwise(q, k, v, i_pre, f_pre):
    B, S, H, Dh = q.shape
    nc = S // CHUNK
    q = q.reshape(B, nc, CHUNK, H, Dh)
    k = k.reshape(B, nc, CHUNK, H, Dh)
    v = v.reshape(B, nc, CHUNK, H, Dh)
    ig = i_pre.reshape(B, nc, CHUNK, H)
    a = jnp.cumsum(jax.nn.log_sigmoid(f_pre).reshape(B, nc, CHUNK, H), axis=2)
    A = a[:, :, -1]
    w_end = A[:, :, None] - a + ig
    m_loc = w_end.max(axis=2)
    e = jnp.exp(w_end - m_loc[:, :, None])
    C_loc = jnp.einsum('bnlh,bnlhk,bnlhv->bnhkv', e, k, v)
    n_loc = jnp.einsum('bnlh,bnlhk->bnhk', e, k)

    def step(carry, xs):
        C, n, m = carry
        A_c, m_l, C_l, n_l = xs
        m_new = jnp.maximum(A_c + m, m_l)
        s_old = jnp.exp(A_c + m - m_new)
        s_loc = jnp.exp(m_l - m_new)
        C_new = s_old[..., None, None] * C + s_loc[..., None, None] * C_l
        n_new = s_old[..., None] * n + s_loc[..., None] * n_l
        return (C_new, n_new, m_new), (C, n, m)

    init = (jnp.zeros((B, H, Dh, Dh), jnp.float32), jnp.zeros((B, H, Dh), jnp.float32),
            jnp.full((B, H), NEG, jnp.float32))
    xs = (jnp.moveaxis(A, 1, 0), jnp.moveaxis(m_loc, 1, 0), jnp.moveaxis(C_loc, 1, 0), jnp.moveaxis(n_loc, 1, 0))
    _, (C_prev, n_prev, m_prev) = lax.scan(step, init, xs)
    C_prev = jnp.moveaxis(C_prev, 0, 1)
    n_prev = jnp.moveaxis(n_prev, 0, 1)
    m_prev = jnp.moveaxis(m_prev, 0, 1)
    D = a[:, :, :, None, :] - a[:, :, None, :, :] + ig[:, :, None, :, :]
    causal = np.tril(np.ones((CHUNK, CHUNK), dtype=bool))
    D = jnp.where(jnp.asarray(causal)[None, None, :, :, None], D, NEG)
    m_inter = a + m_prev[:, :, None]
    m_t = jnp.maximum(m_inter, D.max(axis=3))
    qk = jnp.einsum('bnthd,bnshd->bntsh', q, k) * jnp.exp(D - m_t[:, :, :, None])
    inter = jnp.exp(m_inter - m_t)
    num = jnp.einsum('bntsh,bnshd->bnthd', qk, v) + inter[..., None] * jnp.einsum('bnthk,bnhkv->bnthv', q, C_prev)
    den = qk.sum(axis=3) + inter * jnp.einsum('bnthk,bnhk->bnth', q, n_prev)
    h = num / jnp.maximum(jnp.abs(den), jnp.exp(-m_t))[..., None]
    return h.reshape(B, S, H, Dh)


def hybrid_layer(x, g_ffn1, w_ffn1_gu, w_ffn1_down, g_mix, w_in, w_conv, b_gates, attn_sink,
                 g_mlstm_out, w_out, g_ffn2, w_ffn2_gu, w_ffn2_down, rel_table):
    B, S, _ = x.shape
    f32 = jnp.float32
    h = x + 0.5 * swiglu(rmsnorm(x, g_ffn1), w_ffn1_gu, w_ffn1_down)
    u = rmsnorm(h, g_mix)
    proj = u @ w_in
    q_a, k_a, v_a, q_m, k_m, v_m, o_m, gate_pre = jnp.split(proj, np.cumsum(SPLIT_SIZES)[:-1].tolist(), axis=-1)
    attn = banded_attention(q_a.reshape(B, S, N_ATTN_HEADS, HEAD_DIM),
                            k_a.reshape(B, S, N_KV_HEADS, HEAD_DIM),
                            v_a.reshape(B, S, N_KV_HEADS, HEAD_DIM), rel_table, attn_sink)
    qk_m = jax.nn.silu(centred_dwconv(jnp.concatenate([q_m, k_m], axis=-1), w_conv)).astype(f32)
    qm = qk_m[..., :MLSTM_WIDTH].reshape(B, S, N_MLSTM_HEADS, HEAD_DIM)
    km = qk_m[..., MLSTM_WIDTH:].reshape(B, S, N_MLSTM_HEADS, HEAD_DIM) * (HEAD_DIM ** -0.5)
    vm = v_m.astype(f32).reshape(B, S, N_MLSTM_HEADS, HEAD_DIM)
    gates = gate_pre.astype(f32).reshape(B, S, 4, N_MLSTM_HEADS) + b_gates.astype(f32)
    h_fwd = mlstm_chunkwise(qm, km, vm, gates[:, :, 0], gates[:, :, 1])
    flip = lambda t: jnp.flip(t, axis=1)
    h_bwd = flip(mlstm_chunkwise(flip(qm), flip(km), flip(vm), flip(gates[:, :, 2]), flip(gates[:, :, 3])))
    hm = jax.nn.sigmoid(o_m.astype(f32)).reshape(B, S, N_MLSTM_HEADS, HEAD_DIM) * (h_fwd + h_bwd)
    hm = hm * lax.rsqrt(jnp.mean(hm * hm, axis=-1, keepdims=True) + EPS)
    hm = hm * g_mlstm_out.astype(f32).reshape(N_MLSTM_HEADS, HEAD_DIM)
    mix = jnp.concatenate([attn, hm.reshape(B, S, MLSTM_WIDTH)], axis=-1).astype(x.dtype) @ w_out
    h = h + mix
    h = h + 0.5 * swiglu(rmsnorm(h, g_ffn2), w_ffn2_gu, w_ffn2_down)
    return h


def trunk(x, layer_params, rel_table, g_final):
    h = x
    for l in range(DEPTH):
        h = hybrid_layer(h, *[p[l] for p in layer_params], rel_table)
    return rmsnorm(h, g_final)


def setup_inputs(seed: int = 0) -> dict:
    key = jax.random.key(seed)
    ks = jax.random.split(key, 20)
    nrm = lambda k, shape, s: jax.random.normal(k, shape, jnp.float32) * s
    gain = lambda k, shape: 1.0 + nrm(k, shape, 0.02)
    forget_bias = jnp.linspace(3.0, 6.0, N_MLSTM_HEADS, dtype=jnp.float32)
    gb = nrm(ks[8], (DEPTH, 4, N_MLSTM_HEADS), 0.1)
    b_gates = gb + jnp.stack([jnp.zeros_like(forget_bias), forget_bias,
                              jnp.zeros_like(forget_bias), forget_bias])[None]
    return {
        "x_prompt": nrm(ks[0], (BATCH, SEQ, D_MODEL), 1.0),
        "x_sample": nrm(ks[1], (DEC_BATCH, DEC_SEQ, D_MODEL), 1.0),
        "g_ffn1": gain(ks[2], (DEPTH, D_MODEL)),
        "w_ffn1_gu": nrm(ks[3], (DEPTH, D_MODEL, 2 * D_FF), D_MODEL ** -0.5),
        "w_ffn1_down": nrm(ks[4], (DEPTH, D_FF, D_MODEL), D_FF ** -0.5),
        "g_mix": gain(ks[5], (DEPTH, D_MODEL)),
        "w_in": nrm(ks[6], (DEPTH, D_MODEL, IN_WIDTH), D_MODEL ** -0.5),
        "w_conv": nrm(ks[7], (DEPTH, CONV_WIDTH, 2 * MLSTM_WIDTH), CONV_WIDTH ** -0.5),
        "b_gates": b_gates,
        "attn_sink": nrm(ks[9], (DEPTH, N_ATTN_HEADS), 0.5),
        "g_mlstm_out": gain(ks[10], (DEPTH, MLSTM_WIDTH)),
        "w_out": nrm(ks[11], (DEPTH, MIX_WIDTH, D_MODEL), MIX_WIDTH ** -0.5),
        "g_ffn2": gain(ks[12], (DEPTH, D_MODEL)),
        "w_ffn2_gu": nrm(ks[13], (DEPTH, D_MODEL, 2 * D_FF), D_MODEL ** -0.5),
        "w_ffn2_down": nrm(ks[14], (DEPTH, D_FF, D_MODEL), D_FF ** -0.5),
        "rel_bias_table": nrm(ks[15], (N_BUCKETS, N_ATTN_HEADS), 0.2),
        "g_final": gain(ks[16], (D_MODEL,)),
    }


def reference(x_prompt, x_sample, g_ffn1, w_ffn1_gu, w_ffn1_down, g_mix, w_in, w_conv, b_gates,
              attn_sink, g_mlstm_out, w_out, g_ffn2, w_ffn2_gu, w_ffn2_down, rel_bias_table, g_final):
    layer_params = (g_ffn1, w_ffn1_gu, w_ffn1_down, g_mix, w_in, w_conv, b_gates, attn_sink,
                    g_mlstm_out, w_out, g_ffn2, w_ffn2_gu, w_ffn2_down)
    y_prompt = trunk(x_prompt, layer_params, rel_bias_table, g_final)
    y_sample = trunk(x_sample, layer_params, rel_bias_table, g_final)
    return (y_prompt, y_sample)
```

```python
import functools
import math

import jax
import jax.numpy as jnp
import numpy as np
from jax import lax
from jax.experimental import pallas as pl
from jax.experimental.pallas import tpu as pltpu

D_MODEL = 1024
HEAD_DIM = 64
N_ATTN_HEADS = 8
N_KV_HEADS = 2
GQA_GROUP = N_ATTN_HEADS // N_KV_HEADS
ATTN_WIDTH = N_ATTN_HEADS * HEAD_DIM
KV_WIDTH = N_KV_HEADS * HEAD_DIM
N_MLSTM_HEADS = 8
MLSTM_WIDTH = N_MLSTM_HEADS * HEAD_DIM
MIX_WIDTH = ATTN_WIDTH + MLSTM_WIDTH
WINDOW = 128
BLOCK = 128
N_BUCKETS = 32
MAX_DISTANCE = 128
CHUNK = 128
CONV_WIDTH = 5
D_FF = 2816
EPS = 1e-6
NEG = -1e30
SPLIT_SIZES = (ATTN_WIDTH, KV_WIDTH, KV_WIDTH, MLSTM_WIDTH, MLSTM_WIDTH, MLSTM_WIDTH,
               MLSTM_WIDTH, 4 * N_MLSTM_HEADS)

LANES = 128
BF16_SUBLANES = 16

PROJ_QK_M = 0
PROJ_V_M = 2 * MLSTM_WIDTH
PROJ_O_M = 3 * MLSTM_WIDTH
PROJ_Q_A = 4 * MLSTM_WIDTH
PROJ_K_A = PROJ_Q_A + ATTN_WIDTH
PROJ_V_A = PROJ_K_A + KV_WIDTH
PROJ_WIDTH = PROJ_V_A + KV_WIDTH
GATE_PAD = LANES
HALO = BF16_SUBLANES

TOKEN_TILE = 256
FF_CHUNK = D_FF // 2
VMEM_LIMIT = 56 * 1024 * 1024

_F32 = jnp.float32
_BF16 = jnp.bfloat16


def _rmsnorm(x, g):
    y = x * lax.rsqrt(jnp.mean(x * x, axis=-1, keepdims=True) + EPS)
    return y * g


def _swiglu(xn, wgu_ref, wd_ref):
    acc = None
    for lo in range(0, D_FF, FF_CHUNK):
        g = jnp.dot(xn, wgu_ref[:, lo:lo + FF_CHUNK], preferred_element_type=_F32)
        u = jnp.dot(xn, wgu_ref[:, D_FF + lo:D_FF + lo + FF_CHUNK], preferred_element_type=_F32)
        act = (jax.nn.silu(g) * u).astype(_BF16)
        part = jnp.dot(act, wd_ref[lo:lo + FF_CHUNK, :], preferred_element_type=_F32)
        acc = part if acc is None else acc + part
    return acc


def _ffn_proj_kernel(x_ref, g1_ref, wgu_ref, wd_ref, gmix_ref, win_ref, wgate_ref, bgate_ref,
                     h_ref, proj_ref, gate_ref):
    x = x_ref[...]
    xn = _rmsnorm(x, g1_ref[...]).astype(_BF16)
    h = x + 0.5 * _swiglu(xn, wgu_ref, wd_ref)
    h_ref[...] = h
    un = _rmsnorm(h, gmix_ref[...]).astype(_BF16)
    proj_ref[...] = jnp.dot(un, win_ref[...], preferred_element_type=_F32).astype(_BF16)
    gate_ref[...] = jnp.dot(un, wgate_ref[...], preferred_element_type=_F32) + bgate_ref[...]


def _const_spec(shape):
    nd = len(shape)
    return pl.BlockSpec(shape, lambda *_: (0,) * nd, pipeline_mode=pl.Buffered(1))


def _ffn_proj(x2, g1, wgu, wd, gmix, win, wgate, bgate):
    n = x2.shape[0]
    tm = TOKEN_TILE
    row = lambda w: pl.BlockSpec((tm, w), lambda i: (i, 0))
    return pl.pallas_call(
        _ffn_proj_kernel,
        grid=(n // tm,),
        in_specs=[row(D_MODEL), _const_spec((1, D_MODEL)), _const_spec((D_MODEL, 2 * D_FF)),
                  _const_spec((D_FF, D_MODEL)), _const_spec((1, D_MODEL)),
                  _const_spec((D_MODEL, PROJ_WIDTH)), _const_spec((D_MODEL, GATE_PAD)),
                  _const_spec((1, GATE_PAD))],
        out_specs=[row(D_MODEL), row(PROJ_WIDTH), row(GATE_PAD)],
        out_shape=[jax.ShapeDtypeStruct((n, D_MODEL), _F32),
                   jax.ShapeDtypeStruct((n, PROJ_WIDTH), _BF16),
                   jax.ShapeDtypeStruct((n, GATE_PAD), _F32)],
        compiler_params=pltpu.CompilerParams(dimension_semantics=("parallel",),
                                             vmem_limit_bytes=VMEM_LIMIT),
        name="ffn1_proj",
    )(x2, g1, wgu, wd, gmix, win, wgate, bgate)


def _out_ffn_kernel(h_ref, attn_ref, hm_ref, wout_ref, g2_ref, wgu_ref, wd_ref, gf_ref, y_ref):
    h = (h_ref[...]
         + jnp.dot(attn_ref[...], wout_ref[0:ATTN_WIDTH, :], preferred_element_type=_F32)
         + jnp.dot(hm_ref[...], wout_ref[ATTN_WIDTH:MIX_WIDTH, :], preferred_element_type=_F32))
    hn = _rmsnorm(h, g2_ref[...]).astype(_BF16)
    h = h + 0.5 * _swiglu(hn, wgu_ref, wd_ref)
    y_ref[...] = _rmsnorm(h, gf_ref[...])


def _out_ffn(h2, attn2, hm2, wout, g2, wgu, wd, gf):
    n = h2.shape[0]
    tm = TOKEN_TILE
    row = lambda w: pl.BlockSpec((tm, w), lambda i: (i, 0))
    return pl.pallas_call(
        _out_ffn_kernel,
        grid=(n // tm,),
        in_specs=[row(D_MODEL), row(ATTN_WIDTH), row(MLSTM_WIDTH),
                  _const_spec((MIX_WIDTH, D_MODEL)), _const_spec((1, D_MODEL)),
                  _const_spec((D_MODEL, 2 * D_FF)), _const_spec((D_FF, D_MODEL)),
                  _const_spec((1, D_MODEL))],
        out_specs=row(D_MODEL),
        out_shape=jax.ShapeDtypeStruct((n, D_MODEL), _F32),
        compiler_params=pltpu.CompilerParams(dimension_semantics=("parallel",),
                                             vmem_limit_bytes=VMEM_LIMIT),
        name="out_ffn2",
    )(h2, attn2, hm2, wout, g2, wgu, wd, gf)


def _attn_kernel(sink_ref, q_ref, kp_ref, kc_ref, kn_ref, vp_ref, vc_ref, vn_ref, bias_ref, o_ref):
    i = pl.program_id(1)
    nb = pl.num_programs(1)
    qi = lax.broadcasted_iota(jnp.int32, (BLOCK, 3 * BLOCK), 0)
    kj = lax.broadcasted_iota(jnp.int32, (BLOCK, 3 * BLOCK), 1)
    rel = kj - BLOCK - qi
    mask = ((jnp.abs(rel) <= WINDOW)
            & ((kj >= BLOCK) | (i > 0))
            & ((kj < 2 * BLOCK) | (i < nb - 1)))
    k = jnp.concatenate([kp_ref[...], kc_ref[...], kn_ref[...]], axis=0)
    v = jnp.concatenate([vp_ref[...], vc_ref[...], vn_ref[...]], axis=0)
    q = q_ref[...] * (HEAD_DIM ** -0.5)
    outs = []
    for h in range(N_ATTN_HEADS):
        g = h // GQA_GROUP
        qh = q[:, h * HEAD_DIM:(h + 1) * HEAD_DIM]
        kg = k[:, g * HEAD_DIM:(g + 1) * HEAD_DIM]
        vg = v[:, g * HEAD_DIM:(g + 1) * HEAD_DIM]
        s = lax.dot_general(qh, kg, (((1,), (1,)), ((), ())), preferred_element_type=_F32)
        s = jnp.where(mask, s + bias_ref[h], NEG)
        sk = sink_ref[h]
        m = jnp.maximum(jnp.max(s, axis=-1, keepdims=True), sk)
        p = jnp.exp(s - m)
        den = jnp.sum(p, axis=-1, keepdims=True) + jnp.exp(sk - m)
        outs.append(jnp.dot(p.astype(_BF16), vg, preferred_element_type=_F32) / den)
    o_ref[...] = jnp.concatenate(outs, axis=-1).astype(_BF16)


def _attention(proj3, bias, sink):
    b, s, _ = proj3.shape
    nb = s // BLOCK
    kvw = KV_WIDTH
    cur = lambda col: (lambda bi, i: (bi, i, col))
    prev = lambda col: (lambda bi, i: (bi, jnp.maximum(i - 1, 0), col))
    nxt = lambda col: (lambda bi, i: (bi, jnp.minimum(i + 1, nb - 1), col))
    kcol, vcol = PROJ_K_A // kvw, PROJ_V_A // kvw
    kv = lambda f: pl.BlockSpec((None, BLOCK, kvw), f)
    return pl.pallas_call(
        _attn_kernel,
        grid=(b, nb),
        in_specs=[pl.BlockSpec(memory_space=pltpu.SMEM),
                  pl.BlockSpec((None, BLOCK, ATTN_WIDTH), cur(PROJ_Q_A // ATTN_WIDTH)),
                  kv(prev(kcol)), kv(cur(kcol)), kv(nxt(kcol)),
                  kv(prev(vcol)), kv(cur(vcol)), kv(nxt(vcol)),
                  pl.BlockSpec((N_ATTN_HEADS, BLOCK, 3 * BLOCK), lambda bi, i: (0, 0, 0))],
        out_specs=pl.BlockSpec((None, BLOCK, ATTN_WIDTH), lambda bi, i: (bi, i, 0)),
        out_shape=jax.ShapeDtypeStruct((b, s, ATTN_WIDTH), _BF16),
        compiler_params=pltpu.CompilerParams(dimension_semantics=("parallel", "parallel")),
        name="banded_attn",
    )(sink, proj3, proj3, proj3, proj3, proj3, proj3, proj3, bias)


def _mlstm_chunk(q, kt, v, gates, c_ref, m_ref, reverse):
    nh = N_MLSTM_HEADS
    gt = gates.T
    base = 2 * nh if reverse else 0
    ig = gt[base:base + nh, :]
    fg = gt[base + nh:base + 2 * nh, :]
    lf = jnp.minimum(fg, 0.0) - jnp.log1p(jnp.exp(-jnp.abs(fg)))
    lane8 = lax.broadcasted_iota(jnp.int32, (nh, CHUNK), 1)
    a = lf
    sh = 1
    while sh < CHUNK:
        if reverse:
            a = a + jnp.where(lane8 < CHUNK - sh, pltpu.roll(a, CHUNK - sh, 1), 0.0)
        else:
            a = a + jnp.where(lane8 >= sh, pltpu.roll(a, sh, 1), 0.0)
        sh *= 2
    a_end = a[:, 0:1] if reverse else a[:, CHUNK - 1:CHUNK]
    r = ig - a
    w_end = a_end + r
    m_loc = jnp.max(w_end, axis=1, keepdims=True)
    e = jnp.exp(w_end - m_loc)
    m_prev = m_ref[...]
    m_inter = a + m_prev
    m_new = jnp.maximum(a_end + m_prev, m_loc)
    s_old = jnp.exp(a_end + m_prev - m_new)
    s_loc = jnp.exp(m_loc - m_new)
    m_ref[...] = m_new
    stats_t = jnp.concatenate(
        [a, m_inter, jnp.zeros((CHUNK - 2 * nh, CHUNK), _F32)], axis=0).T

    row = lax.broadcasted_iota(jnp.int32, (CHUNK, CHUNK), 0)
    lane = lax.broadcasted_iota(jnp.int32, (CHUNK, CHUNK), 1)
    tri = (lane >= row) if reverse else (lane <= row)
    out = []
    for p in range(nh // 2):
        q_pair = q[:, p * LANES:(p + 1) * LANES]
        kt_pair = kt[p * LANES:(p + 1) * LANES, :]
        v_pair = v[:, p * LANES:(p + 1) * LANES].astype(_F32)
        nd = []
        for par in range(2):
            h = 2 * p + par
            lo = par * HEAD_DIM
            ones_lane = HEAD_DIM - lo
            in_head_row = (row >= lo) & (row < lo + HEAD_DIM)
            in_head_lane = (lane >= lo) & (lane < lo + HEAD_DIM)
            ktz = jnp.where(in_head_row, kt_pair, 0.0)
            vaug = jnp.where(in_head_lane, v_pair,
                             jnp.where(lane == ones_lane, 1.0, 0.0)).astype(_BF16)
            a_col = stats_t[:, h:h + 1]
            mi_col = stats_t[:, nh + h:nh + h + 1]
            d = jnp.where(tri, a_col + r[h:h + 1, :], NEG)
            m_t = jnp.maximum(mi_col, jnp.max(d, axis=1, keepdims=True))
            pw = jnp.exp(d - m_t)
            inter = jnp.exp(mi_col - m_t)
            s = jnp.dot(q_pair, ktz.astype(_BF16), preferred_element_type=_F32)
            qk = (s * pw).astype(_BF16)
            cz = c_ref[h]
            ndh = (jnp.dot(qk, vaug, preferred_element_type=_F32)
                   + inter * jnp.dot(q_pair, cz.astype(_BF16), preferred_element_type=_F32))
            den = ndh[:, ones_lane:ones_lane + 1]
            nd.append(ndh / jnp.maximum(jnp.abs(den), jnp.exp(-m_t)))
            ket = (ktz * e[h:h + 1, :]).astype(_BF16)
            c_loc = jnp.dot(ket, vaug, preferred_element_type=_F32)
            c_ref[h] = s_old[h:h + 1, :] * cz + s_loc[h:h + 1, :] * c_loc
        out.append(jnp.where(lane < HEAD_DIM, nd[0], nd[1]))
    return out


def _reset_state(c_ref, m_ref):
    @pl.when(pl.program_id(1) == 0)
    def _():
        c_ref[...] = jnp.zeros_like(c_ref)
        m_ref[...] = jnp.full_like(m_ref, NEG)


def _mlstm_fwd_kernel(xm_ref, xp_ref, xn_ref, wconv_ref, v_ref, gate_ref,
                      hf_ref, q_ref, kt_ref, c_ref, m_ref):
    c = pl.program_id(1)
    nc = pl.num_programs(1)
    _reset_state(c_ref, m_ref)
    xprev = jnp.where(c > 0, xp_ref[...].astype(_F32), 0.0)
    xnext = jnp.where(c < nc - 1, xn_ref[...].astype(_F32), 0.0)
    xpad = jnp.concatenate([xprev, xm_ref[...].astype(_F32), xnext], axis=0)
    rows = CHUNK + 2 * HALO
    w = wconv_ref[...]
    y = None
    for j in range(CONV_WIDTH):
        shift = (CONV_WIDTH // 2 - j) % rows
        xs = xpad if shift == 0 else pltpu.roll(xpad, shift, 0)
        term = xs[HALO:HALO + CHUNK, :] * w[j:j + 1, :]
        y = term if y is None else y + term
    y = jax.nn.silu(y)
    q = y[:, :MLSTM_WIDTH].astype(_BF16)
    kt_bf = (y[:, MLSTM_WIDTH:] * (HEAD_DIM ** -0.5)).T.astype(_BF16)
    q_ref[...] = q
    kt_ref[...] = kt_bf
    out = _mlstm_chunk(q, kt_bf.astype(_F32), v_ref[...], gate_ref[...], c_ref, m_ref, False)
    for p, tile in enumerate(out):
        hf_ref[:, p * LANES:(p + 1) * LANES] = tile


def _mlstm_bwd_kernel(q_ref, kt_ref, v_ref, o_ref, gate_ref, hf_ref, gout_ref,
                      hm_ref, c_ref, m_ref):
    _reset_state(c_ref, m_ref)
    out = _mlstm_chunk(q_ref[...], kt_ref[...].astype(_F32), v_ref[...], gate_ref[...],
                       c_ref, m_ref, True)
    lane = lax.broadcasted_iota(jnp.int32, (CHUNK, LANES), 1)
    lo_half = lane < HEAD_DIM
    for p, tile in enumerate(out):
        sl = slice(p * LANES, (p + 1) * LANES)
        hm = jax.nn.sigmoid(o_ref[:, sl].astype(_F32)) * (hf_ref[:, sl] + tile)
        sq = hm * hm
        ss_lo = jnp.sum(jnp.where(lo_half, sq, 0.0), axis=1, keepdims=True)
        ss_hi = jnp.sum(jnp.where(lo_half, 0.0, sq), axis=1, keepdims=True)
        ms = jnp.where(lo_half, ss_lo, ss_hi) * (1.0 / HEAD_DIM)
        hm_ref[:, sl] = (hm * lax.rsqrt(ms + EPS) * gout_ref[:, sl]).astype(_BF16)


_MLSTM_SCRATCH = [pltpu.VMEM((N_MLSTM_HEADS, CHUNK, LANES), _F32),
                  pltpu.VMEM((N_MLSTM_HEADS, LANES), _F32)]
_MLSTM_PARAMS = pltpu.CompilerParams(dimension_semantics=("parallel", "arbitrary"))


def _mlstm_fwd(proj3, gates3, wconv):
    b, s, _ = proj3.shape
    nc = s // CHUNK
    per = CHUNK // HALO
    nhalo = s // HALO
    qkw = 2 * MLSTM_WIDTH
    return pl.pallas_call(
        _mlstm_fwd_kernel,
        grid=(b, nc),
        in_specs=[pl.BlockSpec((None, CHUNK, qkw), lambda bi, c: (bi, c, PROJ_QK_M // qkw)),
                  pl.BlockSpec((None, HALO, qkw),
                               lambda bi, c: (bi, jnp.maximum(c * per - 1, 0), PROJ_QK_M // qkw)),
                  pl.BlockSpec((None, HALO, qkw),
                               lambda bi, c: (bi, jnp.minimum((c + 1) * per, nhalo - 1),
                                              PROJ_QK_M // qkw)),
                  pl.BlockSpec((8, qkw), lambda bi, c: (0, 0)),
                  pl.BlockSpec((None, CHUNK, MLSTM_WIDTH),
                               lambda bi, c: (bi, c, PROJ_V_M // MLSTM_WIDTH)),
                  pl.BlockSpec((None, CHUNK, GATE_PAD), lambda bi, c: (bi, c, 0))],
        out_specs=[pl.BlockSpec((None, CHUNK, MLSTM_WIDTH), lambda bi, c: (bi, c, 0)),
                   pl.BlockSpec((None, CHUNK, MLSTM_WIDTH), lambda bi, c: (bi, c, 0)),
                   pl.BlockSpec((None, MLSTM_WIDTH, CHUNK), lambda bi, c: (bi, 0, c))],
        out_shape=[jax.ShapeDtypeStruct((b, s, MLSTM_WIDTH), _F32),
                   jax.ShapeDtypeStruct((b, s, MLSTM_WIDTH), _BF16),
                   jax.ShapeDtypeStruct((b, MLSTM_WIDTH, s), _BF16)],
        scratch_shapes=_MLSTM_SCRATCH,
        compiler_params=_MLSTM_PARAMS,
        name="mlstm_fwd",
    )(proj3, proj3, proj3, wconv, proj3, gates3)


def _mlstm_bwd(q3, kt3, proj3, gates3, hf3, gout):
    b, s, _ = proj3.shape
    nc = s // CHUNK
    rev = lambda col: (lambda bi, c: (bi, nc - 1 - c, col))
    tok = lambda col: pl.BlockSpec((None, CHUNK, MLSTM_WIDTH), rev(col))
    return pl.pallas_call(
        _mlstm_bwd_kernel,
        grid=(b, nc),
        in_specs=[tok(0),
                  pl.BlockSpec((None, MLSTM_WIDTH, CHUNK), lambda bi, c: (bi, 0, nc - 1 - c)),
                  tok(PROJ_V_M // MLSTM_WIDTH), tok(PROJ_O_M // MLSTM_WIDTH),
                  pl.BlockSpec((None, CHUNK, GATE_PAD), rev(0)),
                  tok(0),
                  pl.BlockSpec((1, MLSTM_WIDTH), lambda bi, c: (0, 0))],
        out_specs=tok(0),
        out_shape=jax.ShapeDtypeStruct((b, s, MLSTM_WIDTH), _BF16),
        scratch_shapes=_MLSTM_SCRATCH,
        compiler_params=_MLSTM_PARAMS,
        name="mlstm_bwd",
    )(q3, kt3, proj3, proj3, gates3, hf3, gout)


def _t5_bucket(rel):
    nb = N_BUCKETS // 2
    ret = (rel > 0).astype(np.int32) * nb
    n = np.abs(rel)
    max_exact = nb // 2
    large = max_exact + (np.log(np.maximum(n, 1) / max_exact)
                         / math.log(MAX_DISTANCE / max_exact) * (nb - max_exact)).astype(np.int32)
    large = np.minimum(large, nb - 1)
    return (ret + np.where(n < max_exact, n, large)).astype(np.int32)


def _prepare_params(g_ffn1, w_ffn1_gu, w_ffn1_down, g_mix, w_in, w_conv, b_gates, attn_sink,
                    g_mlstm_out, w_out, g_ffn2, w_ffn2_gu, w_ffn2_down, rel_table, g_final):
    row = lambda g: g.reshape(1, -1).astype(_F32)
    offs = np.cumsum((0,) + SPLIT_SIZES)
    q_a, k_a, v_a, q_m, k_m, v_m, o_m, gate = (w_in[0][:, offs[i]:offs[i + 1]] for i in range(8))
    win = jnp.concatenate([q_m, k_m, v_m, o_m, q_a, k_a, v_a], axis=1).astype(_BF16)
    ngate = 4 * N_MLSTM_HEADS
    wgate = jnp.pad(gate, ((0, 0), (0, GATE_PAD - ngate))).astype(_BF16)
    bgate = jnp.pad(b_gates[0].reshape(1, ngate).astype(_F32), ((0, 0), (0, GATE_PAD - ngate)))
    wconv = jnp.pad(w_conv[0].astype(_F32), ((0, 8 - CONV_WIDTH), (0, 0)))
    qi = np.arange(BLOCK)[:, None]
    kj = np.arange(3 * BLOCK)[None, :]
    bias = jnp.transpose(rel_table.astype(_F32)[_t5_bucket((kj - BLOCK) - qi)], (2, 0, 1))
    return dict(
        g1=row(g_ffn1[0]), wgu1=w_ffn1_gu[0].astype(_BF16), wd1=w_ffn1_down[0].astype(_BF16),
        gmix=row(g_mix[0]), win=win, wgate=wgate, bgate=bgate, wconv=wconv,
        sink=attn_sink[0].astype(_F32), bias=bias, gout=row(g_mlstm_out[0]),
        wout=w_out[0].astype(_BF16), g2=row(g_ffn2[0]), wgu2=w_ffn2_gu[0].astype(_BF16),
        wd2=w_ffn2_down[0].astype(_BF16), gf=row(g_final))


def _trunk(x, p):
    b, s, d = x.shape
    n = b * s
    h2, proj2, gates2 = _ffn_proj(x.reshape(n, d), p["g1"], p["wgu1"], p["wd1"], p["gmix"],
                                  p["win"], p["wgate"], p["bgate"])
    proj3 = proj2.reshape(b, s, PROJ_WIDTH)
    gates3 = gates2.reshape(b, s, GATE_PAD)
    attn3 = _attention(proj3, p["bias"], p["sink"])
    hf3, q3, kt3 = _mlstm_fwd(proj3, gates3, p["wconv"])
    hm3 = _mlstm_bwd(q3, kt3, proj3, gates3, hf3, p["gout"])
    y2 = _out_ffn(h2, attn3.reshape(n, ATTN_WIDTH), hm3.reshape(n, MLSTM_WIDTH), p["wout"],
                  p["g2"], p["wgu2"], p["wd2"], p["gf"])
    return y2.reshape(b, s, d)


def kernel(x_prompt, x_sample, g_ffn1, w_ffn1_gu, w_ffn1_down, g_mix, w_in, w_conv, b_gates,
           attn_sink, g_mlstm_out, w_out, g_ffn2, w_ffn2_gu, w_ffn2_down, rel_bias_table, g_final):
    p = _prepare_params(g_ffn1, w_ffn1_gu, w_ffn1_down, g_mix, w_in, w_conv, b_gates, attn_sink,
                        g_mlstm_out, w_out, g_ffn2, w_ffn2_gu, w_ffn2_down, rel_bias_table, g_final)
    return (_trunk(x_prompt, p), _trunk(x_sample, p))
```

```python
import math

import jax
import jax.numpy as jnp
import numpy as np
from jax import lax
from jax.experimental import pallas as pl
from jax.experimental.pallas import tpu as pltpu

D_MODEL = 1024
HEAD_DIM = 64
N_ATTN_HEADS = 8
N_KV_HEADS = 2
GQA_GROUP = N_ATTN_HEADS // N_KV_HEADS
ATTN_WIDTH = N_ATTN_HEADS * HEAD_DIM
KV_WIDTH = N_KV_HEADS * HEAD_DIM
N_MLSTM_HEADS = 8
MLSTM_WIDTH = N_MLSTM_HEADS * HEAD_DIM
MIX_WIDTH = ATTN_WIDTH + MLSTM_WIDTH
WINDOW = 128
BLOCK = 128
N_BUCKETS = 32
MAX_DISTANCE = 128
CHUNK = 128
CONV_WIDTH = 5
D_FF = 2816
EPS = 1e-6
NEG = -1e30
SPLIT_SIZES = (ATTN_WIDTH, KV_WIDTH, KV_WIDTH, MLSTM_WIDTH, MLSTM_WIDTH, MLSTM_WIDTH,
               MLSTM_WIDTH, 4 * N_MLSTM_HEADS)
N_GATES = 4 * N_MLSTM_HEADS

LANES = 128
SUBLANES = 8
BF16_SUBLANES = 16

PN_QK_M = 0
PN_O_M = 2 * MLSTM_WIDTH
PN_K_A = PN_O_M + MLSTM_WIDTH
PN_WIDTH = PN_K_A + KV_WIDTH
PT_Q_A = 0
PT_V_M = ATTN_WIDTH
PT_V_A = PT_V_M + MLSTM_WIDTH
PT_ROWS = PT_V_A + KV_WIDTH
GATE_ROWS_PAD = LANES

HALO = BF16_SUBLANES
TOKEN_TILE = 256
PREP_TILE = 512
FF_CHUNK = D_FF // 2
VMEM_LIMIT = 56 * 1024 * 1024

ST_A, ST_GL, ST_E, ST_AEND, ST_MLOC = (i * N_MLSTM_HEADS for i in range(5))
ST_ROWS = 5 * N_MLSTM_HEADS
DEC_K = LANES // 2

_F32 = jnp.float32
_BF16 = jnp.bfloat16


def _rmsnorm(x, g):
    y = x * lax.rsqrt(jnp.mean(x * x, axis=-1, keepdims=True) + EPS)
    return y * g


def _swiglu(xn, wgu_ref, wd_ref):
    acc = None
    for lo in range(0, D_FF, FF_CHUNK):
        g = jnp.dot(xn, wgu_ref[:, lo:lo + FF_CHUNK], preferred_element_type=_F32)
        u = jnp.dot(xn, wgu_ref[:, D_FF + lo:D_FF + lo + FF_CHUNK], preferred_element_type=_F32)
        act = (jax.nn.silu(g) * u).astype(_BF16)
        part = jnp.dot(act, wd_ref[lo:lo + FF_CHUNK, :], preferred_element_type=_F32)
        acc = part if acc is None else acc + part
    return acc


def _split3(x):
    hi = x.astype(_BF16).astype(_F32)
    rem = x - hi
    mid = rem.astype(_BF16).astype(_F32)
    lo = (rem - mid).astype(_BF16).astype(_F32)
    return hi, mid, lo


def _const_spec(shape):
    nd = len(shape)
    return pl.BlockSpec(shape, lambda *_: (0,) * nd, pipeline_mode=pl.Buffered(1))


def _ffn_proj_kernel(x_ref, g1_ref, wgu_ref, wd_ref, gmix_ref, wn_ref, wt_ref, bgate_ref,
                     h_ref, pn_ref, pt_ref, gt_ref):
    x = x_ref[...]
    xn = _rmsnorm(x, g1_ref[...]).astype(_BF16)
    h = x + 0.5 * _swiglu(xn, wgu_ref, wd_ref)
    h_ref[...] = h
    un = _rmsnorm(h, gmix_ref[...]).astype(_BF16)
    pn_ref[...] = jnp.dot(un, wn_ref[...], preferred_element_type=_F32).astype(_BF16)
    t = lax.dot_general(wt_ref[...], un, (((1,), (1,)), ((), ())), preferred_element_type=_F32)
    pt_ref[...] = t[:PT_ROWS].astype(_BF16)
    gt_ref[...] = t[PT_ROWS:PT_ROWS + N_GATES] + bgate_ref[...]


def _ffn_proj(x2, seq, g1, wgu, wd, gmix, wn, wt, bgate):
    n = x2.shape[0]
    tm = TOKEN_TILE
    per_seq = seq // tm
    row = lambda w: pl.BlockSpec((tm, w), lambda i: (i, 0))
    col = lambda r: pl.BlockSpec((None, r, tm), lambda i: (i // per_seq, 0, i % per_seq))
    return pl.pallas_call(
        _ffn_proj_kernel,
        grid=(n // tm,),
        in_specs=[row(D_MODEL), _const_spec((1, D_MODEL)), _const_spec((D_MODEL, 2 * D_FF)),
                  _const_spec((D_FF, D_MODEL)), _const_spec((1, D_MODEL)),
                  _const_spec((D_MODEL, PN_WIDTH)),
                  _const_spec((PT_ROWS + GATE_ROWS_PAD, D_MODEL)),
                  _const_spec((N_GATES, 1))],
        out_specs=[row(D_MODEL), row(PN_WIDTH), col(PT_ROWS), col(N_GATES)],
        out_shape=[jax.ShapeDtypeStruct((n, D_MODEL), _F32),
                   jax.ShapeDtypeStruct((n, PN_WIDTH), _BF16),
                   jax.ShapeDtypeStruct((n // seq, PT_ROWS, seq), _BF16),
                   jax.ShapeDtypeStruct((n // seq, N_GATES, seq), _F32)],
        compiler_params=pltpu.CompilerParams(dimension_semantics=("parallel",),
                                             vmem_limit_bytes=VMEM_LIMIT),
        name="ffn1_proj",
    )(x2, g1, wgu, wd, gmix, wn, wt, bgate)


def _out_ffn_kernel(h_ref, attn_ref, hf_ref, hb_ref, o_ref, gout_ref, wout_ref, g2_ref,
                    wgu_ref, wd_ref, gf_ref, y_ref):
    tm = h_ref.shape[0]
    lane = lax.broadcasted_iota(jnp.int32, (tm, LANES), 1)
    lo_half = lane < HEAD_DIM
    hm_tiles = []
    for p in range(N_MLSTM_HEADS // 2):
        sl = slice(p * LANES, (p + 1) * LANES)
        hm = (jax.nn.sigmoid(o_ref[:, sl].astype(_F32))
              * (hf_ref[:, sl].astype(_F32) + hb_ref[:, sl].astype(_F32)))
        sq = hm * hm
        ss_lo = jnp.sum(jnp.where(lo_half, sq, 0.0), axis=1, keepdims=True)
        ss_hi = jnp.sum(jnp.where(lo_half, 0.0, sq), axis=1, keepdims=True)
        ms = jnp.where(lo_half, ss_lo, ss_hi) * (1.0 / HEAD_DIM)
        hm_tiles.append((hm * lax.rsqrt(ms + EPS) * gout_ref[:, sl]).astype(_BF16))
    hm_all = jnp.concatenate(hm_tiles, axis=1)
    h = (h_ref[...]
         + jnp.dot(attn_ref[...], wout_ref[0:ATTN_WIDTH, :], preferred_element_type=_F32)
         + jnp.dot(hm_all, wout_ref[ATTN_WIDTH:MIX_WIDTH, :], preferred_element_type=_F32))
    hn = _rmsnorm(h, g2_ref[...]).astype(_BF16)
    h = h + 0.5 * _swiglu(hn, wgu_ref, wd_ref)
    y_ref[...] = _rmsnorm(h, gf_ref[...])


def _out_ffn(h2, attn2, hf2, hb2, pn2, gout, wout, g2, wgu, wd, gf):
    n = h2.shape[0]
    tm = TOKEN_TILE
    row = lambda w, c=0: pl.BlockSpec((tm, w), lambda i: (i, c))
    return pl.pallas_call(
        _out_ffn_kernel,
        grid=(n // tm,),
        in_specs=[row(D_MODEL), row(ATTN_WIDTH), row(MLSTM_WIDTH), row(MLSTM_WIDTH),
                  row(MLSTM_WIDTH, PN_O_M // MLSTM_WIDTH),
                  _const_spec((1, MLSTM_WIDTH)), _const_spec((MIX_WIDTH, D_MODEL)),
                  _const_spec((1, D_MODEL)), _const_spec((D_MODEL, 2 * D_FF)),
                  _const_spec((D_FF, D_MODEL)), _const_spec((1, D_MODEL))],
        out_specs=row(D_MODEL),
        out_shape=jax.ShapeDtypeStruct((n, D_MODEL), _F32),
        compiler_params=pltpu.CompilerParams(dimension_semantics=("parallel",),
                                             vmem_limit_bytes=VMEM_LIMIT),
        name="out_ffn2",
    )(h2, attn2, hf2, hb2, pn2, gout, wout, g2, wgu, wd, gf)


def _attn_kernel(sink_ref, qt_ref, kp_ref, kc_ref, kn_ref, vp_ref, vc_ref, vn_ref, bias_ref,
                 o_ref):
    i = pl.program_id(1)
    nb = pl.num_programs(1)
    kj = lax.broadcasted_iota(jnp.int32, (3 * BLOCK, BLOCK), 0)
    qi = lax.broadcasted_iota(jnp.int32, (3 * BLOCK, BLOCK), 1)
    rel = kj - BLOCK - qi
    mask = ((jnp.abs(rel) <= WINDOW)
            & ((kj >= BLOCK) | (i > 0))
            & ((kj < 2 * BLOCK) | (i < nb - 1)))
    k = jnp.concatenate([kp_ref[...], kc_ref[...], kn_ref[...]], axis=0)
    vt = jnp.concatenate([vp_ref[...], vc_ref[...], vn_ref[...]], axis=1)
    zeros = jnp.zeros((HEAD_DIM, BLOCK), _BF16)
    scores = []
    for h in range(N_ATTN_HEADS):
        g = h // GQA_GROUP
        qt = qt_ref[h * HEAD_DIM:(h + 1) * HEAD_DIM, :] * (HEAD_DIM ** -0.5)
        qtz = jnp.concatenate([qt, zeros] if g == 0 else [zeros, qt], axis=0)
        scores.append(jnp.dot(k, qtz, preferred_element_type=_F32))
    probs, dens = [], []
    for h in range(N_ATTN_HEADS):
        s = jnp.where(mask, scores[h] + bias_ref[h], NEG)
        sk = sink_ref[h]
        m = jnp.maximum(jnp.max(s, axis=0, keepdims=True), sk)
        p = jnp.exp(s - m)
        dens.append(jnp.sum(p, axis=0, keepdims=True) + jnp.exp(sk - m))
        probs.append(p.astype(_BF16))
    outs = []
    for h in range(N_ATTN_HEADS):
        g = h // GQA_GROUP
        vtg = vt[g * HEAD_DIM:(g + 1) * HEAD_DIM, :]
        outs.append(jnp.dot(vtg, probs[h], preferred_element_type=_F32) / dens[h])
    o_ref[...] = jnp.concatenate(outs, axis=0).T.astype(_BF16)


def _attention(pn3, pt3, bias_t, sink):
    b, s, _ = pn3.shape
    nb = s // BLOCK
    kcol = PN_K_A // KV_WIDTH
    vrow = PT_V_A // KV_WIDTH
    prev = lambda i: jnp.maximum(i - 1, 0)
    nxt = lambda i: jnp.minimum(i + 1, nb - 1)
    kspec = lambda f: pl.BlockSpec((None, BLOCK, KV_WIDTH), lambda bi, i: (bi, f(i), kcol))
    vspec = lambda f: pl.BlockSpec((None, KV_WIDTH, BLOCK), lambda bi, i: (bi, vrow, f(i)))
    ident = lambda i: i
    return pl.pallas_call(
        _attn_kernel,
        grid=(b, nb),
        in_specs=[pl.BlockSpec(memory_space=pltpu.SMEM),
                  pl.BlockSpec((None, ATTN_WIDTH, BLOCK),
                               lambda bi, i: (bi, PT_Q_A // ATTN_WIDTH, i)),
                  kspec(prev), kspec(ident), kspec(nxt),
                  vspec(prev), vspec(ident), vspec(nxt),
                  pl.BlockSpec((N_ATTN_HEADS, 3 * BLOCK, BLOCK), lambda bi, i: (0, 0, 0))],
        out_specs=pl.BlockSpec((None, BLOCK, ATTN_WIDTH), lambda bi, i: (bi, i, 0)),
        out_shape=jax.ShapeDtypeStruct((b, s, ATTN_WIDTH), _BF16),
        compiler_params=pltpu.CompilerParams(dimension_semantics=("parallel", "parallel")),
        name="banded_attn",
    )(sink, pt3, pn3, pn3, pn3, pt3, pt3, pt3, bias_t)


def _mlstm_prep_kernel(xm_ref, xp_ref, xn_ref, wconv_ref, gt_ref,
                       k_ref, qt_ref, st_ref, dec_ref):
    j = pl.program_id(1)
    nt = pl.num_programs(1)
    nh = N_MLSTM_HEADS
    tp = xm_ref.shape[0]
    xprev = jnp.where(j > 0, xp_ref[...].astype(_F32), 0.0)
    xnext = jnp.where(j < nt - 1, xn_ref[...].astype(_F32), 0.0)
    xpad = jnp.concatenate([xprev, xm_ref[...].astype(_F32), xnext], axis=0)
    rows = tp + 2 * HALO
    w = wconv_ref[...]
    y = None
    for tap in range(CONV_WIDTH):
        shift = (CONV_WIDTH // 2 - tap) % rows
        xs = xpad if shift == 0 else pltpu.roll(xpad, shift, 0)
        term = xs[HALO:HALO + tp, :] * w[tap:tap + 1, :]
        y = term if y is None else y + term
    y = jax.nn.silu(y)
    k_ref[...] = (y[:, MLSTM_WIDTH:] * (HEAD_DIM ** -0.5)).astype(_BF16)
    qt_ref[...] = y[:, :MLSTM_WIDTH].T.astype(_BF16)

    srow = lax.broadcasted_iota(jnp.int32, (CHUNK, CHUNK), 0)
    slane = lax.broadcasted_iota(jnp.int32, (CHUNK, CHUNK), 1)
    tri = [(srow <= slane).astype(_BF16), (srow >= slane).astype(_BF16)]
    lane8 = lax.broadcasted_iota(jnp.int32, (nh, CHUNK), 1)
    ones = jnp.ones((3 * nh, CHUNK), _F32)
    zpad = jnp.zeros((DEC_K - 6 * nh, CHUNK), _F32)
    for c in range(tp // CHUNK):
        cs = slice(c * CHUNK, (c + 1) * CHUNK)
        gt = gt_ref[:, cs]
        dec_rows = []
        for d in range(2):
            ig = gt[2 * d * nh:(2 * d + 1) * nh, :]
            fg = gt[(2 * d + 1) * nh:(2 * d + 2) * nh, :]
            lf = jnp.minimum(fg, 0.0) - jnp.log1p(jnp.exp(-jnp.abs(fg)))
            hi, mid, lo = _split3(lf)
            parts = jnp.concatenate([hi, mid, lo], axis=0).astype(_BF16)
            sums = jnp.dot(parts, tri[d], preferred_element_type=_F32)
            a = sums[0:nh] + sums[nh:2 * nh] + sums[2 * nh:3 * nh]
            r = ig - a
            gl = r
            sh = 1
            while sh < CHUNK:
                if d == 0:
                    gl = jnp.maximum(gl, jnp.where(lane8 >= sh, pltpu.roll(gl, sh, 1), NEG))
                else:
                    gl = jnp.maximum(
                        gl, jnp.where(lane8 < CHUNK - sh, pltpu.roll(gl, CHUNK - sh, 1), NEG))
                sh *= 2
            far = CHUNK - 1 if d == 0 else 0
            a_end = jnp.broadcast_to(a[:, far:far + 1], (nh, CHUNK))
            m_loc = a_end + jnp.broadcast_to(gl[:, far:far + 1], (nh, CHUNK))
            e = jnp.exp(a_end + r - m_loc)
            st_ref[d * ST_ROWS:(d + 1) * ST_ROWS, cs] = jnp.concatenate(
                [a, gl, e, a_end, m_loc], axis=0)
            dec_rows += list(_split3(r)) + [ones, zpad]
        dec_ref[cs, :] = jnp.concatenate(dec_rows, axis=0).T.astype(_BF16)


def _mlstm_prep(pn3, gt3, wconv):
    b, s, _ = pn3.shape
    tp = min(PREP_TILE, s)
    nt = s // tp
    per = tp // HALO
    nhalo = s // HALO
    qkw = 2 * MLSTM_WIDTH
    return pl.pallas_call(
        _mlstm_prep_kernel,
        grid=(b, nt),
        in_specs=[pl.BlockSpec((None, tp, qkw), lambda bi, j: (bi, j, PN_QK_M // qkw)),
                  pl.BlockSpec((None, HALO, qkw),
                               lambda bi, j: (bi, jnp.maximum(j * per - 1, 0), PN_QK_M // qkw)),
                  pl.BlockSpec((None, HALO, qkw),
                               lambda bi, j: (bi, jnp.minimum((j + 1) * per, nhalo - 1),
                                              PN_QK_M // qkw)),
                  pl.BlockSpec((SUBLANES, qkw), lambda bi, j: (0, 0)),
                  pl.BlockSpec((None, N_GATES, tp), lambda bi, j: (bi, 0, j))],
        out_specs=[pl.BlockSpec((None, tp, MLSTM_WIDTH), lambda bi, j: (bi, j, 0)),
                   pl.BlockSpec((None, MLSTM_WIDTH, tp), lambda bi, j: (bi, 0, j)),
                   pl.BlockSpec((None, 2 * ST_ROWS, tp), lambda bi, j: (bi, 0, j)),
                   pl.BlockSpec((None, tp, LANES), lambda bi, j: (bi, j, 0))],
        out_shape=[jax.ShapeDtypeStruct((b, s, MLSTM_WIDTH), _BF16),
                   jax.ShapeDtypeStruct((b, MLSTM_WIDTH, s), _BF16),
                   jax.ShapeDtypeStruct((b, 2 * ST_ROWS, s), _F32),
                   jax.ShapeDtypeStruct((b, s, LANES), _BF16)],
        compiler_params=pltpu.CompilerParams(dimension_semantics=("parallel", "parallel")),
        name="mlstm_prep",
    )(pn3, pn3, pn3, wconv, gt3)


def _mlstm_direction(d, k_ref, qt_ref, vt_ref, dec_ref, st_ref, out_ref, c_ref, m_ref):
    nh = N_MLSTM_HEADS
    st = st_ref[...]
    a = st[ST_A:ST_A + nh]
    gl = st[ST_GL:ST_GL + nh]
    e = st[ST_E:ST_E + nh]
    a_end = st[ST_AEND:ST_AEND + nh]
    m_loc = st[ST_MLOC:ST_MLOC + nh]
    m_prev = m_ref[d]
    g = jnp.maximum(m_prev, gl)
    inter = jnp.exp(m_prev - g)
    floor = jnp.exp(-(a + g))
    ng_hi, ng_mid, ng_lo = _split3(-g)
    m_new = jnp.maximum(a_end + m_prev, m_loc)
    s_old = jnp.exp(a_end + m_prev - m_new)
    s_loc = jnp.exp(m_loc - m_new)
    m_ref[d] = m_new

    row8 = lax.broadcasted_iota(jnp.int32, (nh, CHUNK), 0)
    row = lax.broadcasted_iota(jnp.int32, (CHUNK, CHUNK), 0)
    lane = lax.broadcasted_iota(jnp.int32, (CHUNK, CHUNK), 1)
    keep = (row >= lane) if d else (row <= lane)
    zero8 = jnp.zeros((nh, CHUNK), _F32)
    zero_half = jnp.zeros((DEC_K, CHUNK), _F32)
    dec = dec_ref[...]
    heads = []
    for p in range(nh // 2):
        ps = slice(p * LANES, (p + 1) * LANES)
        qt_pair = qt_ref[ps, :]
        qt_f = qt_pair.astype(_F32)
        vt_f = vt_ref[ps, :].astype(_F32)
        k_pair = k_ref[:, ps]
        k_f = k_pair.astype(_F32)
        for par in range(2):
            h = 2 * p + par
            lo = par * HEAD_DIM
            ones_row = HEAD_DIM - lo
            head_rows = (row >= lo) & (row < lo + HEAD_DIM)
            head_lanes = (lane >= lo) & (lane < lo + HEAD_DIM)
            hot = row8 == h
            one = jnp.where(hot, 1.0, 0.0)
            blk = jnp.concatenate(
                [one, one, one, jnp.where(hot, ng_hi, 0.0), jnp.where(hot, ng_mid, 0.0),
                 jnp.where(hot, ng_lo, 0.0), zero8, zero8], axis=0)
            x = jnp.concatenate([zero_half, blk] if d else [blk, zero_half], axis=0)
            dexp = jnp.dot(dec, x.astype(_BF16), preferred_element_type=_F32)
            qtz = jnp.where(head_rows, qt_f, 0.0).astype(_BF16)
            s = jnp.dot(k_pair, qtz, preferred_element_type=_F32)
            vaug = jnp.where(head_rows, vt_f, jnp.where(row == ones_row, 1.0, 0.0))
            cz = c_ref[d, h]
            nd_state = jnp.dot(cz.astype(_BF16), qt_pair, preferred_element_type=_F32)
            kz = jnp.where(head_lanes, k_f, 0.0).astype(_BF16)
            c_loc = jnp.dot((vaug * e[h:h + 1, :]).astype(_BF16), kz,
                            preferred_element_type=_F32)
            heads.append(dict(h=h, ones_row=ones_row, dexp=dexp, s=s, vaug=vaug.astype(_BF16),
                              cz=cz, nd_state=nd_state, c_loc=c_loc))

    def stage2():
        for hd in heads:
            pw = jnp.exp(jnp.where(keep, hd["dexp"], NEG))
            qk = (hd["s"] * pw).astype(_BF16)
            hd["nd_intra"] = jnp.dot(hd["vaug"], qk, preferred_element_type=_F32)

    def stage3():
        for p in range(nh // 2):
            nd = []
            for hd in heads[2 * p:2 * p + 2]:
                h = hd["h"]
                ndh = hd["nd_intra"] + inter[h:h + 1, :] * hd["nd_state"]
                den = ndh[hd["ones_row"]:hd["ones_row"] + 1, :]
                nd.append(ndh / jnp.maximum(jnp.abs(den), floor[h:h + 1, :]))
                c_ref[d, h] = s_old[h:h + 1, :] * hd["cz"] + s_loc[h:h + 1, :] * hd["c_loc"]
            out_ref[:, p * LANES:(p + 1) * LANES] = (
                jnp.where(row < HEAD_DIM, nd[0], nd[1]).T.astype(_BF16))

    return stage2, stage3


def _mlstm_seq_kernel(kf_ref, qtf_ref, vtf_ref, decf_ref, stf_ref,
                      kb_ref, qtb_ref, vtb_ref, decb_ref, stb_ref,
                      hf_ref, hb_ref, c_ref, m_ref):
    @pl.when(pl.program_id(1) == 0)
    def _():
        c_ref[...] = jnp.zeros_like(c_ref)
        m_ref[...] = jnp.full_like(m_ref, NEG)

    fwd = _mlstm_direction(0, kf_ref, qtf_ref, vtf_ref, decf_ref, stf_ref, hf_ref, c_ref, m_ref)
    bwd = _mlstm_direction(1, kb_ref, qtb_ref, vtb_ref, decb_ref, stb_ref, hb_ref, c_ref, m_ref)
    fwd[0]()
    bwd[0]()
    fwd[1]()
    bwd[1]()


def _mlstm_seq(k3, qt3, pt3, dec3, st3):
    b, s, _ = k3.shape
    nc = s // CHUNK
    vrow = PT_V_M // MLSTM_WIDTH

    def specs(pos, d):
        return [pl.BlockSpec((None, CHUNK, MLSTM_WIDTH), lambda bi, c: (bi, pos(c), 0)),
                pl.BlockSpec((None, MLSTM_WIDTH, CHUNK), lambda bi, c: (bi, 0, pos(c))),
                pl.BlockSpec((None, MLSTM_WIDTH, CHUNK), lambda bi, c: (bi, vrow, pos(c))),
                pl.BlockSpec((None, CHUNK, LANES), lambda bi, c: (bi, pos(c), 0)),
                pl.BlockSpec((None, ST_ROWS, CHUNK), lambda bi, c: (bi, d, pos(c)))]

    fwd = lambda c: c
    bwd = lambda c: nc - 1 - c
    out = lambda pos: pl.BlockSpec((None, CHUNK, MLSTM_WIDTH), lambda bi, c: (bi, pos(c), 0))
    args = (k3, qt3, pt3, dec3, st3)
    return pl.pallas_call(
        _mlstm_seq_kernel,
        grid=(b, nc),
        in_specs=specs(fwd, 0) + specs(bwd, 1),
        out_specs=[out(fwd), out(bwd)],
        out_shape=[jax.ShapeDtypeStruct((b, s, MLSTM_WIDTH), _BF16)] * 2,
        scratch_shapes=[pltpu.VMEM((2, N_MLSTM_HEADS, CHUNK, LANES), _F32),
                        pltpu.VMEM((2, N_MLSTM_HEADS, LANES), _F32)],
        compiler_params=pltpu.CompilerParams(dimension_semantics=("parallel", "arbitrary")),
        name="mlstm_seq",
    )(*args, *args)


def _t5_bucket(rel):
    nb = N_BUCKETS // 2
    ret = (rel > 0).astype(np.int32) * nb
    n = np.abs(rel)
    max_exact = nb // 2
    large = max_exact + (np.log(np.maximum(n, 1) / max_exact)
                         / math.log(MAX_DISTANCE / max_exact) * (nb - max_exact)).astype(np.int32)
    large = np.minimum(large, nb - 1)
    return (ret + np.where(n < max_exact, n, large)).astype(np.int32)


def _prepare_params(g_ffn1, w_ffn1_gu, w_ffn1_down, g_mix, w_in, w_conv, b_gates, attn_sink,
                    g_mlstm_out, w_out, g_ffn2, w_ffn2_gu, w_ffn2_down, rel_table, g_final):
    row = lambda g: g.reshape(1, -1).astype(_F32)
    offs = np.cumsum((0,) + SPLIT_SIZES)
    q_a, k_a, v_a, q_m, k_m, v_m, o_m, gate = (w_in[0][:, offs[i]:offs[i + 1]] for i in range(8))
    wn = jnp.concatenate([q_m, k_m, o_m, k_a], axis=1).astype(_BF16)
    gate_pad = jnp.pad(gate, ((0, 0), (0, GATE_ROWS_PAD - N_GATES)))
    wt = jnp.concatenate([q_a, v_m, v_a, gate_pad], axis=1).T.astype(_BF16)
    bgate = b_gates[0].reshape(N_GATES, 1).astype(_F32)
    wconv = jnp.pad(w_conv[0].astype(_F32), ((0, SUBLANES - CONV_WIDTH), (0, 0)))
    kj = np.arange(3 * BLOCK)[:, None]
    qi = np.arange(BLOCK)[None, :]
    bucket = jnp.asarray(_t5_bucket((kj - BLOCK) - qi).reshape(-1))
    onehot = (bucket[None, :] == jnp.arange(N_BUCKETS)[:, None]).astype(_F32)
    bias_t = jnp.dot(rel_table.astype(_F32).T, onehot, precision=lax.Precision.HIGHEST)
    bias_t = bias_t.reshape(N_ATTN_HEADS, 3 * BLOCK, BLOCK)
    return dict(
        g1=row(g_ffn1[0]), wgu1=w_ffn1_gu[0].astype(_BF16), wd1=w_ffn1_down[0].astype(_BF16),
        gmix=row(g_mix[0]), wn=wn, wt=wt, bgate=bgate, wconv=wconv,
        sink=attn_sink[0].astype(_F32), bias_t=bias_t, gout=row(g_mlstm_out[0]),
        wout=w_out[0].astype(_BF16), g2=row(g_ffn2[0]), wgu2=w_ffn2_gu[0].astype(_BF16),
        wd2=w_ffn2_down[0].astype(_BF16), gf=row(g_final))


def _trunk(x, p):
    b, s, d = x.shape
    n = b * s
    h2, pn2, pt3, gt3 = _ffn_proj(x.reshape(n, d), s, p["g1"], p["wgu1"], p["wd1"], p["gmix"],
                                  p["wn"], p["wt"], p["bgate"])
    pn3 = pn2.reshape(b, s, PN_WIDTH)
    attn3 = _attention(pn3, pt3, p["bias_t"], p["sink"])
    k3, qt3, st3, dec3 = _mlstm_prep(pn3, gt3, p["wconv"])
    hf3, hb3 = _mlstm_seq(k3, qt3, pt3, dec3, st3)
    y2 = _out_ffn(h2, attn3.reshape(n, ATTN_WIDTH), hf3.reshape(n, MLSTM_WIDTH),
                  hb3.reshape(n, MLSTM_WIDTH), pn2, p["gout"], p["wout"], p["g2"], p["wgu2"],
                  p["wd2"], p["gf"])
    return y2.reshape(b, s, d)


def kernel(x_prompt, x_sample, g_ffn1, w_ffn1_gu, w_ffn1_down, g_mix, w_in, w_conv, b_gates,
           attn_sink, g_mlstm_out, w_out, g_ffn2, w_ffn2_gu, w_ffn2_down, rel_bias_table, g_final):
    p = _prepare_params(g_ffn1, w_ffn1_gu, w_ffn1_down, g_mix, w_in, w_conv, b_gates, attn_sink,
                        g_mlstm_out, w_out, g_ffn2, w_ffn2_gu, w_ffn2_down, rel_bias_table, g_final)
    return (_trunk(x_prompt, p), _trunk(x_sample, p))
```

```python
import math

import jax
import jax.numpy as jnp
import numpy as np
from jax import lax
from jax.experimental import pallas as pl
from jax.experimental.pallas import tpu as pltpu

D_MODEL = 1024
HEAD_DIM = 64
N_ATTN_HEADS = 8
N_KV_HEADS = 2
GQA_GROUP = N_ATTN_HEADS // N_KV_HEADS
ATTN_WIDTH = N_ATTN_HEADS * HEAD_DIM
KV_WIDTH = N_KV_HEADS * HEAD_DIM
N_MLSTM_HEADS = 8
MLSTM_WIDTH = N_MLSTM_HEADS * HEAD_DIM
MIX_WIDTH = ATTN_WIDTH + MLSTM_WIDTH
WINDOW = 128
BLOCK = 128
N_BUCKETS = 32
MAX_DISTANCE = 128
CHUNK = 128
CONV_WIDTH = 5
D_FF = 2816
EPS = 1e-6
NEG = -1e30
SPLIT_SIZES = (ATTN_WIDTH, KV_WIDTH, KV_WIDTH, MLSTM_WIDTH, MLSTM_WIDTH, MLSTM_WIDTH,
               MLSTM_WIDTH, 4 * N_MLSTM_HEADS)
N_GATES = 4 * N_MLSTM_HEADS

LANES = 128
SUBLANES = 8
BF16_SUBLANES = 16

PN_QK_M = 0
PN_O_M = 2 * MLSTM_WIDTH
PN_K_A = PN_O_M + MLSTM_WIDTH
PN_WIDTH = PN_K_A + KV_WIDTH
PT_Q_A = 0
PT_V_M = ATTN_WIDTH
PT_V_A = PT_V_M + MLSTM_WIDTH
PT_ROWS = PT_V_A + KV_WIDTH
GATE_ROWS_PAD = LANES

HALO = BF16_SUBLANES
MXU_DIM = 256

TOKEN_TILE = 512
PREP_TILE = 512
FF_SPLIT = (D_FF // MXU_DIM + 1) // 2 * MXU_DIM
FF_CHUNKS = ((0, FF_SPLIT), (FF_SPLIT, D_FF))
VMEM_LIMIT = 56 * 1024 * 1024

ST_A, ST_GL, ST_E, ST_AEND, ST_MLOC = (i * N_MLSTM_HEADS for i in range(5))
ST_ROWS = 5 * N_MLSTM_HEADS
DEC_K = LANES // 2

_F32 = jnp.float32
_BF16 = jnp.bfloat16


def _rmsnorm(x, g):
    y = x * lax.rsqrt(jnp.mean(x * x, axis=-1, keepdims=True) + EPS)
    return y * g


def _swiglu(xn, wgu_ref, wd_ref):
    acc = None
    for lo, hi in FF_CHUNKS:
        g = jnp.dot(xn, wgu_ref[:, lo:hi], preferred_element_type=_F32)
        u = jnp.dot(xn, wgu_ref[:, D_FF + lo:D_FF + hi], preferred_element_type=_F32)
        act = (jax.nn.silu(g) * u).astype(_BF16)
        part = jnp.dot(act, wd_ref[lo:hi, :], preferred_element_type=_F32)
        acc = part if acc is None else acc + part
    return acc


def _split3(x):
    hi = x.astype(_BF16).astype(_F32)
    rem = x - hi
    mid = rem.astype(_BF16).astype(_F32)
    lo = (rem - mid).astype(_BF16).astype(_F32)
    return hi, mid, lo


def _const_spec(shape):
    nd = len(shape)
    return pl.BlockSpec(shape, lambda *_: (0,) * nd, pipeline_mode=pl.Buffered(1))


def _ffn_proj_kernel(x_ref, g1_ref, wgu_ref, wd_ref, gmix_ref, wn_ref, wt_ref, bgate_ref,
                     h_ref, pn_ref, pt_ref, gt_ref):
    x = x_ref[...]
    xn = _rmsnorm(x, g1_ref[...]).astype(_BF16)
    h = x + 0.5 * _swiglu(xn, wgu_ref, wd_ref)
    h_ref[...] = h
    un = _rmsnorm(h, gmix_ref[...]).astype(_BF16)
    pn_ref[...] = jnp.dot(un, wn_ref[...], preferred_element_type=_F32).astype(_BF16)
    t = lax.dot_general(wt_ref[...], un, (((1,), (1,)), ((), ())), preferred_element_type=_F32)
    pt_ref[...] = t[:PT_ROWS].astype(_BF16)
    gt_ref[...] = t[PT_ROWS:PT_ROWS + N_GATES] + bgate_ref[...]


def _ffn_proj(x2, seq, g1, wgu, wd, gmix, wn, wt, bgate):
    n = x2.shape[0]
    tm = TOKEN_TILE
    per_seq = seq // tm
    row = lambda w: pl.BlockSpec((tm, w), lambda i: (i, 0))
    col = lambda r: pl.BlockSpec((None, r, tm), lambda i: (i // per_seq, 0, i % per_seq))
    return pl.pallas_call(
        _ffn_proj_kernel,
        grid=(n // tm,),
        in_specs=[row(D_MODEL), _const_spec((1, D_MODEL)), _const_spec((D_MODEL, 2 * D_FF)),
                  _const_spec((D_FF, D_MODEL)), _const_spec((1, D_MODEL)),
                  _const_spec((D_MODEL, PN_WIDTH)),
                  _const_spec((PT_ROWS + GATE_ROWS_PAD, D_MODEL)),
                  _const_spec((N_GATES, 1))],
        out_specs=[row(D_MODEL), row(PN_WIDTH), col(PT_ROWS), col(N_GATES)],
        out_shape=[jax.ShapeDtypeStruct((n, D_MODEL), _F32),
                   jax.ShapeDtypeStruct((n, PN_WIDTH), _BF16),
                   jax.ShapeDtypeStruct((n // seq, PT_ROWS, seq), _BF16),
                   jax.ShapeDtypeStruct((n // seq, N_GATES, seq), _F32)],
        compiler_params=pltpu.CompilerParams(dimension_semantics=("parallel",),
                                             vmem_limit_bytes=VMEM_LIMIT),
        name="ffn1_proj",
    )(x2, g1, wgu, wd, gmix, wn, wt, bgate)


def _out_ffn_kernel(h_ref, attn_ref, hf_ref, hb_ref, o_ref, gout_ref, wout_ref, g2_ref,
                    wgu_ref, wd_ref, gf_ref, y_ref):
    tm = h_ref.shape[0]
    lane = lax.broadcasted_iota(jnp.int32, (tm, LANES), 1)
    lo_half = lane < HEAD_DIM
    hm_tiles = []
    for p in range(N_MLSTM_HEADS // 2):
        sl = slice(p * LANES, (p + 1) * LANES)
        hm = (jax.nn.sigmoid(o_ref[:, sl].astype(_F32))
              * (hf_ref[:, sl].astype(_F32) + hb_ref[:, sl].astype(_F32)))
        sq = hm * hm
        ss_lo = jnp.sum(jnp.where(lo_half, sq, 0.0), axis=1, keepdims=True)
        ss_hi = jnp.sum(jnp.where(lo_half, 0.0, sq), axis=1, keepdims=True)
        ms = jnp.where(lo_half, ss_lo, ss_hi) * (1.0 / HEAD_DIM)
        hm_tiles.append((hm * lax.rsqrt(ms + EPS) * gout_ref[:, sl]).astype(_BF16))
    hm_all = jnp.concatenate(hm_tiles, axis=1)
    h = (h_ref[...]
         + jnp.dot(attn_ref[...], wout_ref[0:ATTN_WIDTH, :], preferred_element_type=_F32)
         + jnp.dot(hm_all, wout_ref[ATTN_WIDTH:MIX_WIDTH, :], preferred_element_type=_F32))
    hn = _rmsnorm(h, g2_ref[...]).astype(_BF16)
    h = h + 0.5 * _swiglu(hn, wgu_ref, wd_ref)
    y_ref[...] = _rmsnorm(h, gf_ref[...])


def _out_ffn(h2, attn2, hf2, hb2, pn2, gout, wout, g2, wgu, wd, gf):
    n = h2.shape[0]
    tm = TOKEN_TILE
    row = lambda w, c=0: pl.BlockSpec((tm, w), lambda i: (i, c))
    return pl.pallas_call(
        _out_ffn_kernel,
        grid=(n // tm,),
        in_specs=[row(D_MODEL), row(ATTN_WIDTH), row(MLSTM_WIDTH), row(MLSTM_WIDTH),
                  row(MLSTM_WIDTH, PN_O_M // MLSTM_WIDTH),
                  _const_spec((1, MLSTM_WIDTH)), _const_spec((MIX_WIDTH, D_MODEL)),
                  _const_spec((1, D_MODEL)), _const_spec((D_MODEL, 2 * D_FF)),
                  _const_spec((D_FF, D_MODEL)), _const_spec((1, D_MODEL))],
        out_specs=row(D_MODEL),
        out_shape=jax.ShapeDtypeStruct((n, D_MODEL), _F32),
        compiler_params=pltpu.CompilerParams(dimension_semantics=("parallel",),
                                             vmem_limit_bytes=VMEM_LIMIT),
        name="out_ffn2",
    )(h2, attn2, hf2, hb2, pn2, gout, wout, g2, wgu, wd, gf)


def _attn_kernel(sink_ref, qt_ref, kp_ref, kc_ref, kn_ref, vp_ref, vc_ref, vn_ref, bias_ref,
                 o_ref):
    k = jnp.concatenate([kp_ref[...], kc_ref[...], kn_ref[...]], axis=0)
    vt = jnp.concatenate([vp_ref[...], vc_ref[...], vn_ref[...]], axis=1)
    zeros = jnp.zeros((HEAD_DIM, BLOCK), _BF16)
    scores = []
    for h in range(N_ATTN_HEADS):
        g = h // GQA_GROUP
        qt = qt_ref[h * HEAD_DIM:(h + 1) * HEAD_DIM, :] * (HEAD_DIM ** -0.5)
        qtz = jnp.concatenate([qt, zeros] if g == 0 else [zeros, qt], axis=0)
        scores.append(jnp.dot(k, qtz, preferred_element_type=_F32))
    probs, dens = [], []
    for h in range(N_ATTN_HEADS):
        s = scores[h] + bias_ref[h]
        sk = sink_ref[h]
        m = jnp.maximum(jnp.max(s, axis=0, keepdims=True), sk)
        p = jnp.exp(s - m)
        dens.append(jnp.sum(p, axis=0, keepdims=True) + jnp.exp(sk - m))
        probs.append(p.astype(_BF16))
    outs = []
    for h in range(N_ATTN_HEADS):
        g = h // GQA_GROUP
        vtg = vt[g * HEAD_DIM:(g + 1) * HEAD_DIM, :]
        outs.append(jnp.dot(vtg, probs[h], preferred_element_type=_F32) / dens[h])
    o_ref[...] = jnp.concatenate(outs, axis=0).T.astype(_BF16)


def _attention(pn3, pt3, bias_t, sink):
    b, s, _ = pn3.shape
    nb = s // BLOCK
    kcol = PN_K_A // KV_WIDTH
    vrow = PT_V_A // KV_WIDTH
    prev = lambda i: jnp.maximum(i - 1, 0)
    nxt = lambda i: jnp.minimum(i + 1, nb - 1)
    kspec = lambda f: pl.BlockSpec((None, BLOCK, KV_WIDTH), lambda bi, i: (bi, f(i), kcol))
    vspec = lambda f: pl.BlockSpec((None, KV_WIDTH, BLOCK), lambda bi, i: (bi, vrow, f(i)))
    ident = lambda i: i
    if nb == 1:
        variant = lambda i: 3
    else:
        variant = lambda i: jnp.where(i == 0, 0, jnp.where(i == nb - 1, 2, 1))
    return pl.pallas_call(
        _attn_kernel,
        grid=(b, nb),
        in_specs=[pl.BlockSpec(memory_space=pltpu.SMEM),
                  pl.BlockSpec((None, ATTN_WIDTH, BLOCK),
                               lambda bi, i: (bi, PT_Q_A // ATTN_WIDTH, i)),
                  kspec(prev), kspec(ident), kspec(nxt),
                  vspec(prev), vspec(ident), vspec(nxt),
                  pl.BlockSpec((None, N_ATTN_HEADS, 3 * BLOCK, BLOCK),
                               lambda bi, i: (variant(i), 0, 0, 0))],
        out_specs=pl.BlockSpec((None, BLOCK, ATTN_WIDTH), lambda bi, i: (bi, i, 0)),
        out_shape=jax.ShapeDtypeStruct((b, s, ATTN_WIDTH), _BF16),
        compiler_params=pltpu.CompilerParams(dimension_semantics=("parallel", "parallel")),
        name="banded_attn",
    )(sink, pt3, pn3, pn3, pn3, pt3, pt3, pt3, bias_t)


def _mlstm_prep_kernel(xm_ref, xp_ref, xn_ref, wconv_ref, gt_ref,
                       k_ref, qt_ref, st_ref, dec_ref):
    j = pl.program_id(1)
    nt = pl.num_programs(1)
    nh = N_MLSTM_HEADS
    tp = xm_ref.shape[0]
    xprev = jnp.where(j > 0, xp_ref[...].astype(_F32), 0.0)
    xnext = jnp.where(j < nt - 1, xn_ref[...].astype(_F32), 0.0)
    xpad = jnp.concatenate([xprev, xm_ref[...].astype(_F32), xnext], axis=0)
    rows = tp + 2 * HALO
    w = wconv_ref[...]
    y = None
    for tap in range(CONV_WIDTH):
        shift = (CONV_WIDTH // 2 - tap) % rows
        xs = xpad if shift == 0 else pltpu.roll(xpad, shift, 0)
        term = xs[HALO:HALO + tp, :] * w[tap:tap + 1, :]
        y = term if y is None else y + term
    y = jax.nn.silu(y)
    k_ref[...] = (y[:, MLSTM_WIDTH:] * (HEAD_DIM ** -0.5)).astype(_BF16)
    qt_ref[...] = y[:, :MLSTM_WIDTH].T.astype(_BF16)

    srow = lax.broadcasted_iota(jnp.int32, (CHUNK, CHUNK), 0)
    slane = lax.broadcasted_iota(jnp.int32, (CHUNK, CHUNK), 1)
    tri = [(srow <= slane).astype(_BF16), (srow >= slane).astype(_BF16)]
    lane8 = lax.broadcasted_iota(jnp.int32, (nh, CHUNK), 1)
    ones = jnp.ones((3 * nh, CHUNK), _F32)
    zpad = jnp.zeros((DEC_K - 6 * nh, CHUNK), _F32)
    for c in range(tp // CHUNK):
        cs = slice(c * CHUNK, (c + 1) * CHUNK)
        gt = gt_ref[:, cs]
        dec_rows = []
        for d in range(2):
            ig = gt[2 * d * nh:(2 * d + 1) * nh, :]
            fg = gt[(2 * d + 1) * nh:(2 * d + 2) * nh, :]
            lf = jnp.minimum(fg, 0.0) - jnp.log1p(jnp.exp(-jnp.abs(fg)))
            hi, mid, lo = _split3(lf)
            parts = jnp.concatenate([hi, mid, lo], axis=0).astype(_BF16)
            sums = jnp.dot(parts, tri[d], preferred_element_type=_F32)
            a = sums[0:nh] + sums[nh:2 * nh] + sums[2 * nh:3 * nh]
            r = ig - a
            gl = r
            sh = 1
            while sh < CHUNK:
                if d == 0:
                    gl = jnp.maximum(gl, jnp.where(lane8 >= sh, pltpu.roll(gl, sh, 1), NEG))
                else:
                    gl = jnp.maximum(
                        gl, jnp.where(lane8 < CHUNK - sh, pltpu.roll(gl, CHUNK - sh, 1), NEG))
                sh *= 2
            far = CHUNK - 1 if d == 0 else 0
            a_end = jnp.broadcast_to(a[:, far:far + 1], (nh, CHUNK))
            m_loc = a_end + jnp.broadcast_to(gl[:, far:far + 1], (nh, CHUNK))
            e = jnp.exp(a_end + r - m_loc)
            st_ref[d * ST_ROWS:(d + 1) * ST_ROWS, cs] = jnp.concatenate(
                [a, gl, e, a_end, m_loc], axis=0)
            dec_rows += list(_split3(r)) + [ones, zpad]
        dec_ref[cs, :] = jnp.concatenate(dec_rows, axis=0).T.astype(_BF16)


def _mlstm_prep(pn3, gt3, wconv):
    b, s, _ = pn3.shape
    tp = min(PREP_TILE, s)
    nt = s // tp
    per = tp // HALO
    nhalo = s // HALO
    qkw = 2 * MLSTM_WIDTH
    return pl.pallas_call(
        _mlstm_prep_kernel,
        grid=(b, nt),
        in_specs=[pl.BlockSpec((None, tp, qkw), lambda bi, j: (bi, j, PN_QK_M // qkw)),
                  pl.BlockSpec((None, HALO, qkw),
                               lambda bi, j: (bi, jnp.maximum(j * per - 1, 0), PN_QK_M // qkw)),
                  pl.BlockSpec((None, HALO, qkw),
                               lambda bi, j: (bi, jnp.minimum((j + 1) * per, nhalo - 1),
                                              PN_QK_M // qkw)),
                  pl.BlockSpec((SUBLANES, qkw), lambda bi, j: (0, 0)),
                  pl.BlockSpec((None, N_GATES, tp), lambda bi, j: (bi, 0, j))],
        out_specs=[pl.BlockSpec((None, tp, MLSTM_WIDTH), lambda bi, j: (bi, j, 0)),
                   pl.BlockSpec((None, MLSTM_WIDTH, tp), lambda bi, j: (bi, 0, j)),
                   pl.BlockSpec((None, 2 * ST_ROWS, tp), lambda bi, j: (bi, 0, j)),
                   pl.BlockSpec((None, tp, LANES), lambda bi, j: (bi, j, 0))],
        out_shape=[jax.ShapeDtypeStruct((b, s, MLSTM_WIDTH), _BF16),
                   jax.ShapeDtypeStruct((b, MLSTM_WIDTH, s), _BF16),
                   jax.ShapeDtypeStruct((b, 2 * ST_ROWS, s), _F32),
                   jax.ShapeDtypeStruct((b, s, LANES), _BF16)],
        compiler_params=pltpu.CompilerParams(dimension_semantics=("parallel", "parallel")),
        name="mlstm_prep",
    )(pn3, pn3, pn3, wconv, gt3)


def _mlstm_direction(d, k_ref, qt_ref, vt_ref, dec_ref, st_ref, out_ref, c_ref, n_ref, m_ref):
    nh = N_MLSTM_HEADS
    st = st_ref[...]
    a = st[ST_A:ST_A + nh]
    gl = st[ST_GL:ST_GL + nh]
    e = st[ST_E:ST_E + nh]
    a_end = st[ST_AEND:ST_AEND + nh]
    m_loc = st[ST_MLOC:ST_MLOC + nh]
    m_prev = m_ref[d]
    g = jnp.maximum(m_prev, gl)
    inter = jnp.exp(m_prev - g)
    floor = jnp.exp(-(a + g))
    ng_hi, ng_mid, ng_lo = _split3(-g)
    m_new = jnp.maximum(a_end + m_prev, m_loc)
    s_old = jnp.exp(a_end + m_prev - m_new)
    s_loc = jnp.exp(m_loc - m_new)
    m_ref[d] = m_new

    row8 = lax.broadcasted_iota(jnp.int32, (nh, CHUNK), 0)
    row = lax.broadcasted_iota(jnp.int32, (CHUNK, CHUNK), 0)
    lane = lax.broadcasted_iota(jnp.int32, (CHUNK, CHUNK), 1)
    keep = (row >= lane) if d else (row <= lane)
    zero8 = jnp.zeros((nh, CHUNK), _F32)
    zero_half = jnp.zeros((DEC_K, CHUNK), _F32)
    zero_head = jnp.zeros((HEAD_DIM, CHUNK), _BF16)
    dec = dec_ref[...]
    k_all = k_ref[...]
    n_prev = n_ref[d]
    den_state = jnp.dot(n_prev.astype(_BF16), qt_ref[...], preferred_element_type=_F32)
    n_loc = jnp.dot(e.astype(_BF16), k_all, preferred_element_type=_F32)
    heads = []
    for h in range(nh):
        p, par = divmod(h, 2)
        hs = slice(h * HEAD_DIM, (h + 1) * HEAD_DIM)
        ps = slice(p * LANES, (p + 1) * LANES)
        qt_h = qt_ref[hs, :]
        qtz = jnp.concatenate([zero_head, qt_h] if par else [qt_h, zero_head], axis=0)
        hot = row8 == h
        one = jnp.where(hot, 1.0, 0.0)
        blk = jnp.concatenate(
            [one, one, one, jnp.where(hot, ng_hi, 0.0), jnp.where(hot, ng_mid, 0.0),
             jnp.where(hot, ng_lo, 0.0), zero8, zero8], axis=0)
        x = jnp.concatenate([zero_half, blk] if d else [blk, zero_half], axis=0)
        dexp = jnp.dot(dec, x.astype(_BF16), preferred_element_type=_F32)
        k_pair = k_all[:, ps]
        s = jnp.dot(k_pair, qtz, preferred_element_type=_F32)
        vt_h = vt_ref[hs, :]
        cz = c_ref[d, h]
        nd_state = jnp.dot(cz.astype(_BF16), qtz, preferred_element_type=_F32)
        c_loc = jnp.dot((vt_h.astype(_F32) * e[h:h + 1, :]).astype(_BF16), k_pair,
                        preferred_element_type=_F32)
        heads.append(dict(h=h, dexp=dexp, s=s, vt=vt_h, cz=cz, nd_state=nd_state, c_loc=c_loc))

    def stage2():
        for hd in heads:
            pw = jnp.exp(jnp.where(keep, hd["dexp"], NEG))
            qk = hd["s"] * pw
            hd["den_intra"] = jnp.sum(qk, axis=0, keepdims=True)
            hd["nd_intra"] = jnp.dot(hd["vt"], qk.astype(_BF16), preferred_element_type=_F32)

    def stage3():
        for p in range(nh // 2):
            tiles = []
            for hd in heads[2 * p:2 * p + 2]:
                h = hd["h"]
                it = inter[h:h + 1, :]
                den = hd["den_intra"] + it * den_state[h:h + 1, :]
                num = hd["nd_intra"] + it * hd["nd_state"]
                tiles.append(num / jnp.maximum(jnp.abs(den), floor[h:h + 1, :]))
                c_ref[d, h] = s_old[h:h + 1, :] * hd["cz"] + s_loc[h:h + 1, :] * hd["c_loc"]
            out_ref[:, p * LANES:(p + 1) * LANES] = (
                jnp.concatenate(tiles, axis=0).T.astype(_BF16))
        rep = MLSTM_WIDTH // LANES
        head_of_lane = lax.broadcasted_iota(jnp.int32, (nh, MLSTM_WIDTH), 1) // HEAD_DIM
        own = head_of_lane == lax.broadcasted_iota(jnp.int32, (nh, MLSTM_WIDTH), 0)
        n_ref[d] = (jnp.concatenate([s_old] * rep, axis=1) * n_prev
                    + jnp.concatenate([s_loc] * rep, axis=1) * jnp.where(own, n_loc, 0.0))

    return stage2, stage3


def _mlstm_seq_kernel(kf_ref, qtf_ref, vtf_ref, decf_ref, stf_ref,
                      kb_ref, qtb_ref, vtb_ref, decb_ref, stb_ref,
                      hf_ref, hb_ref, c_ref, n_ref, m_ref):
    @pl.when(pl.program_id(1) == 0)
    def _():
        c_ref[...] = jnp.zeros_like(c_ref)
        n_ref[...] = jnp.zeros_like(n_ref)
        m_ref[...] = jnp.full_like(m_ref, NEG)

    fwd = _mlstm_direction(0, kf_ref, qtf_ref, vtf_ref, decf_ref, stf_ref, hf_ref,
                           c_ref, n_ref, m_ref)
    bwd = _mlstm_direction(1, kb_ref, qtb_ref, vtb_ref, decb_ref, stb_ref, hb_ref,
                           c_ref, n_ref, m_ref)
    fwd[0]()
    bwd[0]()
    fwd[1]()
    bwd[1]()


def _mlstm_seq(k3, qt3, pt3, dec3, st3):
    b, s, _ = k3.shape
    nc = s // CHUNK
    vrow = PT_V_M // MLSTM_WIDTH

    def specs(pos, d):
        return [pl.BlockSpec((None, CHUNK, MLSTM_WIDTH), lambda bi, c: (bi, pos(c), 0)),
                pl.BlockSpec((None, MLSTM_WIDTH, CHUNK), lambda bi, c: (bi, 0, pos(c))),
                pl.BlockSpec((None, MLSTM_WIDTH, CHUNK), lambda bi, c: (bi, vrow, pos(c))),
                pl.BlockSpec((None, CHUNK, LANES), lambda bi, c: (bi, pos(c), 0)),
                pl.BlockSpec((None, ST_ROWS, CHUNK), lambda bi, c: (bi, d, pos(c)))]

    fwd = lambda c: c
    bwd = lambda c: nc - 1 - c
    out = lambda pos: pl.BlockSpec((None, CHUNK, MLSTM_WIDTH), lambda bi, c: (bi, pos(c), 0))
    args = (k3, qt3, pt3, dec3, st3)
    return pl.pallas_call(
        _mlstm_seq_kernel,
        grid=(b, nc),
        in_specs=specs(fwd, 0) + specs(bwd, 1),
        out_specs=[out(fwd), out(bwd)],
        out_shape=[jax.ShapeDtypeStruct((b, s, MLSTM_WIDTH), _BF16)] * 2,
        scratch_shapes=[pltpu.VMEM((2, N_MLSTM_HEADS, HEAD_DIM, LANES), _F32),
                        pltpu.VMEM((2, N_MLSTM_HEADS, MLSTM_WIDTH), _F32),
                        pltpu.VMEM((2, N_MLSTM_HEADS, LANES), _F32)],
        compiler_params=pltpu.CompilerParams(dimension_semantics=("parallel", "arbitrary")),
        name="mlstm_seq",
    )(*args, *args)


def _t5_bucket(rel):
    nb = N_BUCKETS // 2
    ret = (rel > 0).astype(np.int32) * nb
    n = np.abs(rel)
    max_exact = nb // 2
    large = max_exact + (np.log(np.maximum(n, 1) / max_exact)
                         / math.log(MAX_DISTANCE / max_exact) * (nb - max_exact)).astype(np.int32)
    large = np.minimum(large, nb - 1)
    return (ret + np.where(n < max_exact, n, large)).astype(np.int32)


def _prepare_params(g_ffn1, w_ffn1_gu, w_ffn1_down, g_mix, w_in, w_conv, b_gates, attn_sink,
                    g_mlstm_out, w_out, g_ffn2, w_ffn2_gu, w_ffn2_down, rel_table, g_final):
    row = lambda g: g.reshape(1, -1).astype(_F32)
    offs = np.cumsum((0,) + SPLIT_SIZES)
    q_a, k_a, v_a, q_m, k_m, v_m, o_m, gate = (w_in[0][:, offs[i]:offs[i + 1]] for i in range(8))
    wn = jnp.concatenate([q_m, k_m, o_m, k_a], axis=1).astype(_BF16)
    gate_pad = jnp.pad(gate, ((0, 0), (0, GATE_ROWS_PAD - N_GATES)))
    wt = jnp.concatenate([q_a, v_m, v_a, gate_pad], axis=1).T.astype(_BF16)
    bgate = b_gates[0].reshape(N_GATES, 1).astype(_F32)
    wconv = jnp.pad(w_conv[0].astype(_F32), ((0, SUBLANES - CONV_WIDTH), (0, 0)))
    kj = np.arange(3 * BLOCK)[:, None]
    qi = np.arange(BLOCK)[None, :]
    bucket = jnp.asarray(_t5_bucket((kj - BLOCK) - qi).reshape(-1))
    onehot = (bucket[None, :] == jnp.arange(N_BUCKETS)[:, None]).astype(_F32)
    bias_t = jnp.dot(rel_table.astype(_F32).T, onehot, precision=lax.Precision.HIGHEST)
    bias_t = bias_t.reshape(N_ATTN_HEADS, 3 * BLOCK, BLOCK)
    window = np.abs((kj - BLOCK) - qi) <= WINDOW
    has_prev = kj >= BLOCK
    has_next = kj < 2 * BLOCK
    valid = np.stack([window & has_prev, window, window & has_next,
                      window & has_prev & has_next])
    bias_t = jnp.where(jnp.asarray(valid)[:, None], bias_t[None], NEG)
    return dict(
        g1=row(g_ffn1[0]), wgu1=w_ffn1_gu[0].astype(_BF16), wd1=w_ffn1_down[0].astype(_BF16),
        gmix=row(g_mix[0]), wn=wn, wt=wt, bgate=bgate, wconv=wconv,
        sink=attn_sink[0].astype(_F32), bias_t=bias_t, gout=row(g_mlstm_out[0]),
        wout=w_out[0].astype(_BF16), g2=row(g_ffn2[0]), wgu2=w_ffn2_gu[0].astype(_BF16),
        wd2=w_ffn2_down[0].astype(_BF16), gf=row(g_final))


def _trunk(x, p):
    b, s, d = x.shape
    n = b * s
    h2, pn2, pt3, gt3 = _ffn_proj(x.reshape(n, d), s, p["g1"], p["wgu1"], p["wd1"], p["gmix"],
                                  p["wn"], p["wt"], p["bgate"])
    pn3 = pn2.reshape(b, s, PN_WIDTH)
    attn3 = _attention(pn3, pt3, p["bias_t"], p["sink"])
    k3, qt3, st3, dec3 = _mlstm_prep(pn3, gt3, p["wconv"])
    hf3, hb3 = _mlstm_seq(k3, qt3, pt3, dec3, st3)
    y2 = _out_ffn(h2, attn3.reshape(n, ATTN_WIDTH), hf3.reshape(n, MLSTM_WIDTH),
                  hb3.reshape(n, MLSTM_WIDTH), pn2, p["gout"], p["wout"], p["g2"], p["wgu2"],
                  p["wd2"], p["gf"])
    return y2.reshape(b, s, d)


def kernel(x_prompt, x_sample, g_ffn1, w_ffn1_gu, w_ffn1_down, g_mix, w_in, w_conv, b_gates,
           attn_sink, g_mlstm_out, w_out, g_ffn2, w_ffn2_gu, w_ffn2_down, rel_bias_table, g_final):
    p = _prepare_params(g_ffn1, w_ffn1_gu, w_ffn1_down, g_mix, w_in, w_conv, b_gates, attn_sink,
                        g_mlstm_out, w_out, g_ffn2, w_ffn2_gu, w_ffn2_down, rel_bias_table, g_final)
    return (_trunk(x_prompt, p), _trunk(x_sample, p))
```

```python
import functools
import math

import jax
import jax.numpy as jnp
import numpy as np
from jax import lax
from jax.experimental import pallas as pl
from jax.experimental.pallas import tpu as pltpu

D_MODEL = 1024
HEAD_DIM = 64
N_ATTN_HEADS = 8
N_KV_HEADS = 2
GQA_GROUP = N_ATTN_HEADS // N_KV_HEADS
ATTN_WIDTH = N_ATTN_HEADS * HEAD_DIM
KV_WIDTH = N_KV_HEADS * HEAD_DIM
N_MLSTM_HEADS = 8
MLSTM_WIDTH = N_MLSTM_HEADS * HEAD_DIM
MIX_WIDTH = ATTN_WIDTH + MLSTM_WIDTH
WINDOW = 128
BLOCK = 128
N_BUCKETS = 32
MAX_DISTANCE = 128
CHUNK = 128
CONV_WIDTH = 5
D_FF = 2816
EPS = 1e-6
NEG = -1e30
SPLIT_SIZES = (ATTN_WIDTH, KV_WIDTH, KV_WIDTH, MLSTM_WIDTH, MLSTM_WIDTH, MLSTM_WIDTH,
               MLSTM_WIDTH, 4 * N_MLSTM_HEADS)
N_GATES = 4 * N_MLSTM_HEADS

LANES = 128
SUBLANES = 8
BF16_SUBLANES = 16
MXU_DIM = 256

PN_QK_M = 0
PN_O_M = 2 * MLSTM_WIDTH
PN_K_A = PN_O_M + MLSTM_WIDTH
PN_WIDTH = PN_K_A + KV_WIDTH
PT_Q_A = 0
PT_V_M = ATTN_WIDTH
PT_V_A = PT_V_M + MLSTM_WIDTH
PT_ROWS = PT_V_A + KV_WIDTH
GATE_ROWS_PAD = LANES

HALO = BF16_SUBLANES
TOKEN_TILE = 512
PREP_TILE = 512
FF_SPLIT = (D_FF // MXU_DIM + 1) // 2 * MXU_DIM
FF_CHUNKS = ((0, FF_SPLIT), (FF_SPLIT, D_FF))
VMEM_LIMIT = 56 * 1024 * 1024

ST_A, ST_GL, ST_E, ST_AEND, ST_MLOC = (i * N_MLSTM_HEADS for i in range(5))
ST_ROWS = 5 * N_MLSTM_HEADS
DEC_K = LANES // 2

_F32 = jnp.float32
_BF16 = jnp.bfloat16


def _rmsnorm(x, g):
    y = x * lax.rsqrt(jnp.mean(x * x, axis=-1, keepdims=True) + EPS)
    return y * g


def _swiglu(xn, wgu_ref, wd_ref):
    acc = None
    for lo, hi in FF_CHUNKS:
        g = jnp.dot(xn, wgu_ref[:, lo:hi], preferred_element_type=_F32)
        u = jnp.dot(xn, wgu_ref[:, D_FF + lo:D_FF + hi], preferred_element_type=_F32)
        act = (jax.nn.silu(g) * u).astype(_BF16)
        part = jnp.dot(act, wd_ref[lo:hi, :], preferred_element_type=_F32)
        acc = part if acc is None else acc + part
    return acc


def _split3(x):
    hi = x.astype(_BF16).astype(_F32)
    rem = x - hi
    mid = rem.astype(_BF16).astype(_F32)
    lo = (rem - mid).astype(_BF16).astype(_F32)
    return hi, mid, lo


def _const_spec(shape):
    nd = len(shape)
    return pl.BlockSpec(shape, lambda *_: (0,) * nd, pipeline_mode=pl.Buffered(1))


def _ffn_proj_kernel(x_ref, g1_ref, wgu_ref, wd_ref, gmix_ref, wn_ref, wt_ref, bgate_ref,
                     h_ref, pn_ref, pt_ref, gt_ref):
    x = x_ref[...]
    xn = _rmsnorm(x, g1_ref[...]).astype(_BF16)
    h = x + 0.5 * _swiglu(xn, wgu_ref, wd_ref)
    h_ref[...] = h
    un = _rmsnorm(h, gmix_ref[...]).astype(_BF16)
    pn_ref[...] = jnp.dot(un, wn_ref[...], preferred_element_type=_F32).astype(_BF16)
    t = lax.dot_general(wt_ref[...], un, (((1,), (1,)), ((), ())), preferred_element_type=_F32)
    pt_ref[...] = t[:PT_ROWS].astype(_BF16)
    gt_ref[...] = t[PT_ROWS:PT_ROWS + N_GATES] + bgate_ref[...]


def _ffn_proj(x2, seq, g1, wgu, wd, gmix, wn, wt, bgate):
    n = x2.shape[0]
    tm = TOKEN_TILE
    per_seq = seq // tm
    row = lambda w: pl.BlockSpec((tm, w), lambda i: (i, 0))
    col = lambda r: pl.BlockSpec((None, r, tm), lambda i: (i // per_seq, 0, i % per_seq))
    return pl.pallas_call(
        _ffn_proj_kernel,
        grid=(n // tm,),
        in_specs=[row(D_MODEL), _const_spec((1, D_MODEL)), _const_spec((D_MODEL, 2 * D_FF)),
                  _const_spec((D_FF, D_MODEL)), _const_spec((1, D_MODEL)),
                  _const_spec((D_MODEL, PN_WIDTH)),
                  _const_spec((PT_ROWS + GATE_ROWS_PAD, D_MODEL)),
                  _const_spec((N_GATES, 1))],
        out_specs=[row(D_MODEL), row(PN_WIDTH), col(PT_ROWS), col(N_GATES)],
        out_shape=[jax.ShapeDtypeStruct((n, D_MODEL), _F32),
                   jax.ShapeDtypeStruct((n, PN_WIDTH), _BF16),
                   jax.ShapeDtypeStruct((n // seq, PT_ROWS, seq), _BF16),
                   jax.ShapeDtypeStruct((n // seq, N_GATES, seq), _F32)],
        compiler_params=pltpu.CompilerParams(dimension_semantics=("parallel",),
                                             vmem_limit_bytes=VMEM_LIMIT),
        name="ffn1_proj",
    )(x2, g1, wgu, wd, gmix, wn, wt, bgate)


def _attn_scores(qt_ref, kcat, blk, h):
    zeros = jnp.zeros((HEAD_DIM, BLOCK), _BF16)
    keys = kcat[blk * BLOCK:(blk + 3) * BLOCK, :]
    qt = (qt_ref[h * HEAD_DIM:(h + 1) * HEAD_DIM, blk * BLOCK:(blk + 1) * BLOCK]
          * (HEAD_DIM ** -0.5))
    qtz = jnp.concatenate([qt, zeros] if h < GQA_GROUP else [zeros, qt], axis=0)
    return jnp.dot(keys, qtz, preferred_element_type=_F32)


def _attn_softmax(s, sink_ref, bias_ref, h, edge_ok):
    s = s + bias_ref[h]
    if edge_ok is not None:
        s = jnp.where(edge_ok, s, NEG)
    sk = sink_ref[h]
    m = jnp.maximum(jnp.max(s, axis=0, keepdims=True), sk)
    p = jnp.exp(s - m)
    den = jnp.sum(p, axis=0, keepdims=True) + jnp.exp(sk - m)
    return p.astype(_BF16), 1.0 / den


def _attn_values(p, rden, vcat, blk, h):
    g = h // GQA_GROUP
    vtg = vcat[g * HEAD_DIM:(g + 1) * HEAD_DIM, blk * BLOCK:(blk + 3) * BLOCK]
    return jnp.dot(vtg, p, preferred_element_type=_F32) * rden


def _mix_ffn_kernel(sink_ref, qt_ref, kp_ref, kc_ref, kn_ref, vp_ref, vc_ref, vn_ref, bias_ref,
                    h_ref, hf_ref, hb_ref, o_ref, gout_ref, wout_ref, g2_ref, wgu_ref, wd_ref,
                    gf_ref, y_ref, attn_ref, *, tiles_per_seq):
    j = pl.program_id(0)
    tm = h_ref.shape[0]
    nblk = tm // BLOCK

    @pl.when(j == 0)
    def _():
        attn_ref[...] = jnp.zeros_like(attn_ref)

    pos = jnp.minimum(j, pl.num_programs(0) - 2) % tiles_per_seq
    kcat = jnp.concatenate([kp_ref[...], kc_ref[...], kn_ref[...]], axis=0)
    vcat = jnp.concatenate([vp_ref[...], vc_ref[...], vn_ref[...]], axis=1)
    krow = lax.broadcasted_iota(jnp.int32, (3 * BLOCK, BLOCK), 0)
    edge = {0: (krow >= BLOCK) | (pos > 0)}
    ok_next = (krow < 2 * BLOCK) | (pos < tiles_per_seq - 1)
    edge[nblk - 1] = ok_next if nblk > 1 else edge[0] & ok_next
    units = [(blk, h) for blk in range(nblk) for h in range(N_ATTN_HEADS)]
    n_slices = D_FF // MXU_DIM
    per = -(-len(units) // n_slices)
    groups = [units[i * per:(i + 1) * per] for i in range(n_slices)]
    attn_out = {}
    scored = [_attn_scores(qt_ref, kcat, blk, hh) for blk, hh in groups[0]]
    weighted = []

    lane = lax.broadcasted_iota(jnp.int32, (tm, LANES), 1)
    lo_half = lane < HEAD_DIM
    hm_tiles = []
    for p in range(N_MLSTM_HEADS // 2):
        sl = slice(p * LANES, (p + 1) * LANES)
        hm = (jax.nn.sigmoid(o_ref[:, sl].astype(_F32))
              * (hf_ref[:, sl].astype(_F32) + hb_ref[:, sl].astype(_F32)))
        sq = hm * hm
        ss_lo = jnp.sum(jnp.where(lo_half, sq, 0.0), axis=1, keepdims=True)
        ss_hi = jnp.sum(jnp.where(lo_half, 0.0, sq), axis=1, keepdims=True)
        ms = jnp.where(lo_half, ss_lo, ss_hi) * (1.0 / HEAD_DIM)
        hm_tiles.append((hm * lax.rsqrt(ms + EPS) * gout_ref[:, sl]).astype(_BF16))
    hm_all = jnp.concatenate(hm_tiles, axis=1)
    h = (h_ref[...]
         + jnp.dot(attn_ref[...], wout_ref[0:ATTN_WIDTH, :], preferred_element_type=_F32)
         + jnp.dot(hm_all, wout_ref[ATTN_WIDTH:MIX_WIDTH, :], preferred_element_type=_F32))
    hn = _rmsnorm(h, g2_ref[...]).astype(_BF16)
    acc = None
    for t in range(n_slices):
        lo, hi = t * MXU_DIM, (t + 1) * MXU_DIM
        upcoming = ([_attn_scores(qt_ref, kcat, blk, hh) for blk, hh in groups[t + 1]]
                    if t + 1 < n_slices else [])
        g = jnp.dot(hn, wgu_ref[:, lo:hi], preferred_element_type=_F32)
        u = jnp.dot(hn, wgu_ref[:, D_FF + lo:D_FF + hi], preferred_element_type=_F32)
        for (blk, hh), p, rden in weighted:
            attn_out[blk, hh] = _attn_values(p, rden, vcat, blk, hh)
        weighted = [((blk, hh),) + _attn_softmax(s, sink_ref, bias_ref, hh, edge.get(blk))
                    for (blk, hh), s in zip(groups[t], scored)]
        act = (jax.nn.silu(g) * u).astype(_BF16)
        part = jnp.dot(act, wd_ref[lo:hi, :], preferred_element_type=_F32)
        acc = part if acc is None else acc + part
        scored = upcoming
    for (blk, hh), p, rden in weighted:
        attn_out[blk, hh] = _attn_values(p, rden, vcat, blk, hh)
    h = h + 0.5 * acc
    y_ref[...] = _rmsnorm(h, gf_ref[...])

    for blk in range(nblk):
        heads = [attn_out[blk, hh] for hh in range(N_ATTN_HEADS)]
        attn_ref[blk * BLOCK:(blk + 1) * BLOCK, :] = (
            jnp.concatenate(heads, axis=0).T.astype(_BF16))


def _mix_ffn(seq, pn2, pt3, h2, hf2, hb2, sink, bias_t, gout, wout, g2, wgu, wd, gf):
    n = h2.shape[0]
    tm = TOKEN_TILE
    nt = n // tm
    per_seq = seq // tm
    per_blk = tm // BLOCK
    nblocks = n // BLOCK
    blk_per_seq = seq // BLOCK
    ta = lambda j: jnp.minimum(j, nt - 1)
    tf = lambda j: jnp.maximum(j - 1, 0)
    kcol = PN_K_A // KV_WIDTH
    vrow = PT_V_A // KV_WIDTH
    kprev = lambda j: jnp.maximum(ta(j) * per_blk - 1, 0)
    knext = lambda j: jnp.minimum((ta(j) + 1) * per_blk, nblocks - 1)
    vmain = lambda j: (ta(j) // per_seq, vrow, ta(j) % per_seq)
    vprev = lambda j: (kprev(j) // blk_per_seq, vrow, kprev(j) % blk_per_seq)
    vnext = lambda j: (knext(j) // blk_per_seq, vrow, knext(j) % blk_per_seq)
    row = lambda w, c=0: pl.BlockSpec((tm, w), lambda j: (tf(j), c))
    return pl.pallas_call(
        functools.partial(_mix_ffn_kernel, tiles_per_seq=per_seq),
        grid=(nt + 1,),
        in_specs=[pl.BlockSpec(memory_space=pltpu.SMEM),
                  pl.BlockSpec((None, ATTN_WIDTH, tm),
                               lambda j: (ta(j) // per_seq, PT_Q_A // ATTN_WIDTH, ta(j) % per_seq)),
                  pl.BlockSpec((BLOCK, KV_WIDTH), lambda j: (kprev(j), kcol)),
                  pl.BlockSpec((tm, KV_WIDTH), lambda j: (ta(j), kcol)),
                  pl.BlockSpec((BLOCK, KV_WIDTH), lambda j: (knext(j), kcol)),
                  pl.BlockSpec((None, KV_WIDTH, BLOCK), vprev),
                  pl.BlockSpec((None, KV_WIDTH, tm), vmain),
                  pl.BlockSpec((None, KV_WIDTH, BLOCK), vnext),
                  _const_spec((N_ATTN_HEADS, 3 * BLOCK, BLOCK)),
                  row(D_MODEL), row(MLSTM_WIDTH), row(MLSTM_WIDTH),
                  row(MLSTM_WIDTH, PN_O_M // MLSTM_WIDTH),
                  _const_spec((1, MLSTM_WIDTH)), _const_spec((MIX_WIDTH, D_MODEL)),
                  _const_spec((1, D_MODEL)), _const_spec((D_MODEL, 2 * D_FF)),
                  _const_spec((D_FF, D_MODEL)), _const_spec((1, D_MODEL))],
        out_specs=row(D_MODEL),
        out_shape=jax.ShapeDtypeStruct((n, D_MODEL), _F32),
        scratch_shapes=[pltpu.VMEM((tm, ATTN_WIDTH), _BF16)],
        compiler_params=pltpu.CompilerParams(dimension_semantics=("arbitrary",),
                                             vmem_limit_bytes=VMEM_LIMIT),
        name="attn_out_ffn2",
    )(sink, pt3, pn2, pn2, pn2, pt3, pt3, pt3, bias_t, h2, hf2, hb2, pn2,
      gout, wout, g2, wgu, wd, gf)


def _mlstm_prep_kernel(xm_ref, xp_ref, xn_ref, wconv_ref, gt_ref,
                       k_ref, qt_ref, st_ref, dec_ref):
    j = pl.program_id(1)
    nt = pl.num_programs(1)
    nh = N_MLSTM_HEADS
    tp = xm_ref.shape[0]
    xprev = jnp.where(j > 0, xp_ref[...].astype(_F32), 0.0)
    xnext = jnp.where(j < nt - 1, xn_ref[...].astype(_F32), 0.0)
    xpad = jnp.concatenate([xprev, xm_ref[...].astype(_F32), xnext], axis=0)
    rows = tp + 2 * HALO
    w = wconv_ref[...]
    y = None
    for tap in range(CONV_WIDTH):
        shift = (CONV_WIDTH // 2 - tap) % rows
        xs = xpad if shift == 0 else pltpu.roll(xpad, shift, 0)
        term = xs[HALO:HALO + tp, :] * w[tap:tap + 1, :]
        y = term if y is None else y + term
    y = jax.nn.silu(y)
    k_ref[...] = (y[:, MLSTM_WIDTH:] * (HEAD_DIM ** -0.5)).astype(_BF16)
    qt_ref[...] = y[:, :MLSTM_WIDTH].T.astype(_BF16)

    srow = lax.broadcasted_iota(jnp.int32, (CHUNK, CHUNK), 0)
    slane = lax.broadcasted_iota(jnp.int32, (CHUNK, CHUNK), 1)
    tri = [(srow <= slane).astype(_BF16), (srow >= slane).astype(_BF16)]
    lane8 = lax.broadcasted_iota(jnp.int32, (nh, CHUNK), 1)
    ones = jnp.ones((3 * nh, CHUNK), _F32)
    zpad = jnp.zeros((DEC_K - 6 * nh, CHUNK), _F32)
    for c in range(tp // CHUNK):
        cs = slice(c * CHUNK, (c + 1) * CHUNK)
        gt = gt_ref[:, cs]
        dec_rows = []
        for d in range(2):
            ig = gt[2 * d * nh:(2 * d + 1) * nh, :]
            fg = gt[(2 * d + 1) * nh:(2 * d + 2) * nh, :]
            lf = jnp.minimum(fg, 0.0) - jnp.log1p(jnp.exp(-jnp.abs(fg)))
            hi, mid, lo = _split3(lf)
            parts = jnp.concatenate([hi, mid, lo], axis=0).astype(_BF16)
            sums = jnp.dot(parts, tri[d], preferred_element_type=_F32)
            a = sums[0:nh] + sums[nh:2 * nh] + sums[2 * nh:3 * nh]
            r = ig - a
            gl = r
            sh = 1
            while sh < CHUNK:
                if d == 0:
                    gl = jnp.maximum(gl, jnp.where(lane8 >= sh, pltpu.roll(gl, sh, 1), NEG))
                else:
                    gl = jnp.maximum(
                        gl, jnp.where(lane8 < CHUNK - sh, pltpu.roll(gl, CHUNK - sh, 1), NEG))
                sh *= 2
            far = CHUNK - 1 if d == 0 else 0
            a_end = jnp.broadcast_to(a[:, far:far + 1], (nh, CHUNK))
            m_loc = a_end + jnp.broadcast_to(gl[:, far:far + 1], (nh, CHUNK))
            e = jnp.exp(a_end + r - m_loc)
            st_ref[d * ST_ROWS:(d + 1) * ST_ROWS, cs] = jnp.concatenate(
                [a, gl, e, a_end, m_loc], axis=0)
            dec_rows += list(_split3(r)) + [ones, zpad]
        dec_ref[cs, :] = jnp.concatenate(dec_rows, axis=0).T.astype(_BF16)


def _mlstm_prep(pn3, gt3, wconv):
    b, s, _ = pn3.shape
    tp = min(PREP_TILE, s)
    nt = s // tp
    per = tp // HALO
    nhalo = s // HALO
    qkw = 2 * MLSTM_WIDTH
    return pl.pallas_call(
        _mlstm_prep_kernel,
        grid=(b, nt),
        in_specs=[pl.BlockSpec((None, tp, qkw), lambda bi, j: (bi, j, PN_QK_M // qkw)),
                  pl.BlockSpec((None, HALO, qkw),
                               lambda bi, j: (bi, jnp.maximum(j * per - 1, 0), PN_QK_M // qkw)),
                  pl.BlockSpec((None, HALO, qkw),
                               lambda bi, j: (bi, jnp.minimum((j + 1) * per, nhalo - 1),
                                              PN_QK_M // qkw)),
                  pl.BlockSpec((SUBLANES, qkw), lambda bi, j: (0, 0)),
                  pl.BlockSpec((None, N_GATES, tp), lambda bi, j: (bi, 0, j))],
        out_specs=[pl.BlockSpec((None, tp, MLSTM_WIDTH), lambda bi, j: (bi, j, 0)),
                   pl.BlockSpec((None, MLSTM_WIDTH, tp), lambda bi, j: (bi, 0, j)),
                   pl.BlockSpec((None, 2 * ST_ROWS, tp), lambda bi, j: (bi, 0, j)),
                   pl.BlockSpec((None, tp, LANES), lambda bi, j: (bi, j, 0))],
        out_shape=[jax.ShapeDtypeStruct((b, s, MLSTM_WIDTH), _BF16),
                   jax.ShapeDtypeStruct((b, MLSTM_WIDTH, s), _BF16),
                   jax.ShapeDtypeStruct((b, 2 * ST_ROWS, s), _F32),
                   jax.ShapeDtypeStruct((b, s, LANES), _BF16)],
        compiler_params=pltpu.CompilerParams(dimension_semantics=("parallel", "parallel")),
        name="mlstm_prep",
    )(pn3, pn3, pn3, wconv, gt3)


def _mlstm_direction(d, k_ref, qt_ref, vt_ref, dec_ref, st_ref, out_ref, c_ref, n_ref, m_ref):
    nh = N_MLSTM_HEADS
    st = st_ref[...]
    a = st[ST_A:ST_A + nh]
    gl = st[ST_GL:ST_GL + nh]
    e = st[ST_E:ST_E + nh]
    a_end = st[ST_AEND:ST_AEND + nh]
    m_loc = st[ST_MLOC:ST_MLOC + nh]
    m_prev = m_ref[d]
    g = jnp.maximum(m_prev, gl)
    inter = jnp.exp(m_prev - g)
    floor = jnp.exp(-(a + g))
    ng_hi, ng_mid, ng_lo = _split3(-g)
    m_new = jnp.maximum(a_end + m_prev, m_loc)
    s_old = jnp.exp(a_end + m_prev - m_new)
    s_loc = jnp.exp(m_loc - m_new)
    m_ref[d] = m_new

    row8 = lax.broadcasted_iota(jnp.int32, (nh, CHUNK), 0)
    row = lax.broadcasted_iota(jnp.int32, (CHUNK, CHUNK), 0)
    lane = lax.broadcasted_iota(jnp.int32, (CHUNK, CHUNK), 1)
    keep = (row >= lane) if d else (row <= lane)
    zero8 = jnp.zeros((nh, CHUNK), _F32)
    zero_half = jnp.zeros((DEC_K, CHUNK), _F32)
    zero_head = jnp.zeros((HEAD_DIM, CHUNK), _BF16)
    dec = dec_ref[...]
    k_all = k_ref[...]
    n_prev = n_ref[d]
    den_state = jnp.dot(n_prev.astype(_BF16), qt_ref[...], preferred_element_type=_F32)
    n_loc = jnp.dot(e.astype(_BF16), k_all, preferred_element_type=_F32)
    heads = []
    for h in range(nh):
        p, par = divmod(h, 2)
        hs = slice(h * HEAD_DIM, (h + 1) * HEAD_DIM)
        ps = slice(p * LANES, (p + 1) * LANES)
        qt_h = qt_ref[hs, :]
        qtz = jnp.concatenate([zero_head, qt_h] if par else [qt_h, zero_head], axis=0)
        hot = row8 == h
        one = jnp.where(hot, 1.0, 0.0)
        blk = jnp.concatenate(
            [one, one, one, jnp.where(hot, ng_hi, 0.0), jnp.where(hot, ng_mid, 0.0),
             jnp.where(hot, ng_lo, 0.0), zero8, zero8], axis=0)
        x = jnp.concatenate([zero_half, blk] if d else [blk, zero_half], axis=0)
        dexp = jnp.dot(dec, x.astype(_BF16), preferred_element_type=_F32)
        k_pair = k_all[:, ps]
        s = jnp.dot(k_pair, qtz, preferred_element_type=_F32)
        vt_h = vt_ref[hs, :]
        cz = c_ref[d, h]
        nd_state = jnp.dot(cz.astype(_BF16), qtz, preferred_element_type=_F32)
        c_loc = jnp.dot((vt_h.astype(_F32) * e[h:h + 1, :]).astype(_BF16), k_pair,
                        preferred_element_type=_F32)
        heads.append(dict(h=h, dexp=dexp, s=s, vt=vt_h, cz=cz, nd_state=nd_state, c_loc=c_loc))

    def stage2():
        for hd in heads:
            pw = jnp.exp(jnp.where(keep, hd["dexp"], NEG))
            qk = hd["s"] * pw
            hd["den_intra"] = jnp.sum(qk, axis=0, keepdims=True)
            hd["nd_intra"] = jnp.dot(hd["vt"], qk.astype(_BF16), preferred_element_type=_F32)

    def stage3():
        for p in range(nh // 2):
            tiles = []
            for hd in heads[2 * p:2 * p + 2]:
                h = hd["h"]
                it = inter[h:h + 1, :]
                den = hd["den_intra"] + it * den_state[h:h + 1, :]
                num = hd["nd_intra"] + it * hd["nd_state"]
                tiles.append(num / jnp.maximum(jnp.abs(den), floor[h:h + 1, :]))
                c_ref[d, h] = s_old[h:h + 1, :] * hd["cz"] + s_loc[h:h + 1, :] * hd["c_loc"]
            out_ref[:, p * LANES:(p + 1) * LANES] = (
                jnp.concatenate(tiles, axis=0).T.astype(_BF16))
        rep = MLSTM_WIDTH // LANES
        head_of_lane = lax.broadcasted_iota(jnp.int32, (nh, MLSTM_WIDTH), 1) // HEAD_DIM
        own = head_of_lane == lax.broadcasted_iota(jnp.int32, (nh, MLSTM_WIDTH), 0)
        n_ref[d] = (jnp.concatenate([s_old] * rep, axis=1) * n_prev
                    + jnp.concatenate([s_loc] * rep, axis=1) * jnp.where(own, n_loc, 0.0))

    return stage2, stage3


def _mlstm_seq_kernel(kf_ref, qtf_ref, vtf_ref, decf_ref, stf_ref,
                      kb_ref, qtb_ref, vtb_ref, decb_ref, stb_ref,
                      hf_ref, hb_ref, c_ref, n_ref, m_ref):
    @pl.when(pl.program_id(1) == 0)
    def _():
        c_ref[...] = jnp.zeros_like(c_ref)
        n_ref[...] = jnp.zeros_like(n_ref)
        m_ref[...] = jnp.full_like(m_ref, NEG)

    fwd = _mlstm_direction(0, kf_ref, qtf_ref, vtf_ref, decf_ref, stf_ref, hf_ref,
                           c_ref, n_ref, m_ref)
    bwd = _mlstm_direction(1, kb_ref, qtb_ref, vtb_ref, decb_ref, stb_ref, hb_ref,
                           c_ref, n_ref, m_ref)
    fwd[0]()
    bwd[0]()
    fwd[1]()
    bwd[1]()


def _mlstm_seq(k3, qt3, pt3, dec3, st3):
    b, s, _ = k3.shape
    nc = s // CHUNK
    vrow = PT_V_M // MLSTM_WIDTH

    def specs(pos, d):
        return [pl.BlockSpec((None, CHUNK, MLSTM_WIDTH), lambda bi, c: (bi, pos(c), 0)),
                pl.BlockSpec((None, MLSTM_WIDTH, CHUNK), lambda bi, c: (bi, 0, pos(c))),
                pl.BlockSpec((None, MLSTM_WIDTH, CHUNK), lambda bi, c: (bi, vrow, pos(c))),
                pl.BlockSpec((None, CHUNK, LANES), lambda bi, c: (bi, pos(c), 0)),
                pl.BlockSpec((None, ST_ROWS, CHUNK), lambda bi, c: (bi, d, pos(c)))]

    fwd = lambda c: c
    bwd = lambda c: nc - 1 - c
    out = lambda pos: pl.BlockSpec((None, CHUNK, MLSTM_WIDTH), lambda bi, c: (bi, pos(c), 0))
    args = (k3, qt3, pt3, dec3, st3)
    return pl.pallas_call(
        _mlstm_seq_kernel,
        grid=(b, nc),
        in_specs=specs(fwd, 0) + specs(bwd, 1),
        out_specs=[out(fwd), out(bwd)],
        out_shape=[jax.ShapeDtypeStruct((b, s, MLSTM_WIDTH), _BF16)] * 2,
        scratch_shapes=[pltpu.VMEM((2, N_MLSTM_HEADS, HEAD_DIM, LANES), _F32),
                        pltpu.VMEM((2, N_MLSTM_HEADS, MLSTM_WIDTH), _F32),
                        pltpu.VMEM((2, N_MLSTM_HEADS, LANES), _F32)],
        compiler_params=pltpu.CompilerParams(dimension_semantics=("parallel", "arbitrary")),
        name="mlstm_seq",
    )(*args, *args)


def _t5_bucket(rel):
    nb = N_BUCKETS // 2
    ret = (rel > 0).astype(np.int32) * nb
    n = np.abs(rel)
    max_exact = nb // 2
    large = max_exact + (np.log(np.maximum(n, 1) / max_exact)
                         / math.log(MAX_DISTANCE / max_exact) * (nb - max_exact)).astype(np.int32)
    large = np.minimum(large, nb - 1)
    return (ret + np.where(n < max_exact, n, large)).astype(np.int32)


def _prepare_params(g_ffn1, w_ffn1_gu, w_ffn1_down, g_mix, w_in, w_conv, b_gates, attn_sink,
                    g_mlstm_out, w_out, g_ffn2, w_ffn2_gu, w_ffn2_down, rel_table, g_final):
    row = lambda g: g.reshape(1, -1).astype(_F32)
    offs = np.cumsum((0,) + SPLIT_SIZES)
    q_a, k_a, v_a, q_m, k_m, v_m, o_m, gate = (w_in[0][:, offs[i]:offs[i + 1]] for i in range(8))
    wn = jnp.concatenate([q_m, k_m, o_m, k_a], axis=1).astype(_BF16)
    gate_pad = jnp.pad(gate, ((0, 0), (0, GATE_ROWS_PAD - N_GATES)))
    wt = jnp.concatenate([q_a, v_m, v_a, gate_pad], axis=1).T.astype(_BF16)
    bgate = b_gates[0].reshape(N_GATES, 1).astype(_F32)
    wconv = jnp.pad(w_conv[0].astype(_F32), ((0, SUBLANES - CONV_WIDTH), (0, 0)))
    kj = np.arange(3 * BLOCK)[:, None]
    qi = np.arange(BLOCK)[None, :]
    bucket = jnp.asarray(_t5_bucket((kj - BLOCK) - qi).reshape(-1))
    onehot = (bucket[None, :] == jnp.arange(N_BUCKETS)[:, None]).astype(_F32)
    bias_t = jnp.dot(rel_table.astype(_F32).T, onehot, precision=lax.Precision.HIGHEST)
    bias_t = bias_t.reshape(N_ATTN_HEADS, 3 * BLOCK, BLOCK)
    window = np.abs((kj - BLOCK) - qi) <= WINDOW
    bias_t = jnp.where(jnp.asarray(window)[None], bias_t, NEG)
    return dict(
        g1=row(g_ffn1[0]), wgu1=w_ffn1_gu[0].astype(_BF16), wd1=w_ffn1_down[0].astype(_BF16),
        gmix=row(g_mix[0]), wn=wn, wt=wt, bgate=bgate, wconv=wconv,
        sink=attn_sink[0].astype(_F32), bias_t=bias_t, gout=row(g_mlstm_out[0]),
        wout=w_out[0].astype(_BF16), g2=row(g_ffn2[0]), wgu2=w_ffn2_gu[0].astype(_BF16),
        wd2=w_ffn2_down[0].astype(_BF16), gf=row(g_final))


def _trunk(x, p):
    b, s, d = x.shape
    n = b * s
    h2, pn2, pt3, gt3 = _ffn_proj(x.reshape(n, d), s, p["g1"], p["wgu1"], p["wd1"], p["gmix"],
                                  p["wn"], p["wt"], p["bgate"])
    pn3 = pn2.reshape(b, s, PN_WIDTH)
    k3, qt3, st3, dec3 = _mlstm_prep(pn3, gt3, p["wconv"])
    hf3, hb3 = _mlstm_seq(k3, qt3, pt3, dec3, st3)
    y2 = _mix_ffn(s, pn2, pt3, h2, hf3.reshape(n, MLSTM_WIDTH), hb3.reshape(n, MLSTM_WIDTH),
                  p["sink"], p["bias_t"], p["gout"], p["wout"], p["g2"], p["wgu2"], p["wd2"],
                  p["gf"])
    return y2.reshape(b, s, d)


def kernel(x_prompt, x_sample, g_ffn1, w_ffn1_gu, w_ffn1_down, g_mix, w_in, w_conv, b_gates,
           attn_sink, g_mlstm_out, w_out, g_ffn2, w_ffn2_gu, w_ffn2_down, rel_bias_table, g_final):
    p = _prepare_params(g_ffn1, w_ffn1_gu, w_ffn1_down, g_mix, w_in, w_conv, b_gates, attn_sink,
                        g_mlstm_out, w_out, g_ffn2, w_ffn2_gu, w_ffn2_down, rel_bias_table, g_final)
    return (_trunk(x_prompt, p), _trunk(x_sample, p))
```

```python
import functools
import math

import jax
import jax.numpy as jnp
import numpy as np
from jax import lax
from jax.experimental import pallas as pl
from jax.experimental.pallas import tpu as pltpu

D_MODEL = 1024
HEAD_DIM = 64
N_ATTN_HEADS = 8
N_KV_HEADS = 2
GQA_GROUP = N_ATTN_HEADS // N_KV_HEADS
ATTN_WIDTH = N_ATTN_HEADS * HEAD_DIM
KV_WIDTH = N_KV_HEADS * HEAD_DIM
N_MLSTM_HEADS = 8
MLSTM_WIDTH = N_MLSTM_HEADS * HEAD_DIM
MIX_WIDTH = ATTN_WIDTH + MLSTM_WIDTH
WINDOW = 128
BLOCK = 128
N_BUCKETS = 32
MAX_DISTANCE = 128
CHUNK = 128
CONV_WIDTH = 5
D_FF = 2816
EPS = 1e-6
NEG = -1e30
SPLIT_SIZES = (ATTN_WIDTH, KV_WIDTH, KV_WIDTH, MLSTM_WIDTH, MLSTM_WIDTH, MLSTM_WIDTH,
               MLSTM_WIDTH, 4 * N_MLSTM_HEADS)
N_GATES = 4 * N_MLSTM_HEADS

LANES = 128
SUBLANES = 8
BF16_SUBLANES = 16
MXU_DIM = 256

PN_O_M = 0
PN_K_A = PN_O_M + MLSTM_WIDTH
PN_WIDTH = PN_K_A + KV_WIDTH
PT_Q_A = 0
PT_V_M = ATTN_WIDTH
PT_V_A = PT_V_M + MLSTM_WIDTH
PT_ROWS = PT_V_A + KV_WIDTH
GATE_ROWS_PAD = LANES

HALO = BF16_SUBLANES
TOKEN_TILE = 512
SEQ_CHUNKS = 4
FF_SPLIT = (D_FF // MXU_DIM + 1) // 2 * MXU_DIM
FF_CHUNKS = ((0, FF_SPLIT), (FF_SPLIT, D_FF))
VMEM_LIMIT = 56 * 1024 * 1024

ST_A, ST_GL, ST_E, ST_AEND, ST_MLOC = (i * N_MLSTM_HEADS for i in range(5))
ST_ROWS = 5 * N_MLSTM_HEADS
DEC_K = LANES // 2

_F32 = jnp.float32
_BF16 = jnp.bfloat16


def _rmsnorm(x, g):
    y = x * lax.rsqrt(jnp.mean(x * x, axis=-1, keepdims=True) + EPS)
    return y * g


def _swiglu(xn, wgu_ref, wd_ref):
    acc = None
    for lo, hi in FF_CHUNKS:
        g = jnp.dot(xn, wgu_ref[:, lo:hi], preferred_element_type=_F32)
        u = jnp.dot(xn, wgu_ref[:, D_FF + lo:D_FF + hi], preferred_element_type=_F32)
        act = (jax.nn.silu(g) * u).astype(_BF16)
        part = jnp.dot(act, wd_ref[lo:hi, :], preferred_element_type=_F32)
        acc = part if acc is None else acc + part
    return acc


def _split3(x):
    hi = x.astype(_BF16).astype(_F32)
    rem = x - hi
    mid = rem.astype(_BF16).astype(_F32)
    lo = (rem - mid).astype(_BF16).astype(_F32)
    return hi, mid, lo


def _const_spec(shape):
    nd = len(shape)
    return pl.BlockSpec(shape, lambda *_: (0,) * nd, pipeline_mode=pl.Buffered(1))


def _conv_unit(qk_ref, wconv_ref, c, r0):
    cs = slice(c * LANES, (c + 1) * LANES)
    w = wconv_ref[:, cs]
    y = None
    for tap in range(CONV_WIDTH):
        lo = r0 + HALO - CONV_WIDTH // 2 + tap
        term = qk_ref[lo:lo + CHUNK, cs] * w[tap:tap + 1, :]
        y = term if y is None else y + term
    return jax.nn.silu(y)


def _gate_scan_start(gt, tri):
    nh = N_MLSTM_HEADS
    out = []
    for d in range(2):
        ig = gt[2 * d * nh:(2 * d + 1) * nh, :]
        fg = gt[(2 * d + 1) * nh:(2 * d + 2) * nh, :]
        lf = jnp.minimum(fg, 0.0) - jnp.log1p(jnp.exp(-jnp.abs(fg)))
        parts = jnp.concatenate(_split3(lf), axis=0).astype(_BF16)
        sums = jnp.dot(parts, tri[d], preferred_element_type=_F32)
        out.append((ig, sums[0:nh] + sums[nh:2 * nh] + sums[2 * nh:3 * nh]))
    return out


def _gate_scan_finish(scans, st_ref, dec_ref, cs):
    nh = N_MLSTM_HEADS
    lane8 = lax.broadcasted_iota(jnp.int32, (nh, CHUNK), 1)
    ones = jnp.ones((3 * nh, CHUNK), _F32)
    zpad = jnp.zeros((DEC_K - 6 * nh, CHUNK), _F32)
    dec_rows = []
    for d, (ig, a) in enumerate(scans):
        r = ig - a
        gl = r
        sh = 1
        while sh < CHUNK:
            if d == 0:
                gl = jnp.maximum(gl, jnp.where(lane8 >= sh, pltpu.roll(gl, sh, 1), NEG))
            else:
                gl = jnp.maximum(
                    gl, jnp.where(lane8 < CHUNK - sh, pltpu.roll(gl, CHUNK - sh, 1), NEG))
            sh *= 2
        far = CHUNK - 1 if d == 0 else 0
        a_end = jnp.broadcast_to(a[:, far:far + 1], (nh, CHUNK))
        m_loc = a_end + jnp.broadcast_to(gl[:, far:far + 1], (nh, CHUNK))
        e = jnp.exp(a_end + r - m_loc)
        st_ref[d * ST_ROWS:(d + 1) * ST_ROWS, cs] = jnp.concatenate(
            [a, gl, e, a_end, m_loc], axis=0)
        dec_rows += list(_split3(r)) + [ones, zpad]
    dec_ref[cs, :] = jnp.concatenate(dec_rows, axis=0).T.astype(_BF16)


def _ffn_proj_kernel(xm_ref, xp_ref, xn_ref, g1_ref, wgu_ref, wd_ref, gmix_ref, wqk_ref, wn_ref,
                     wt_ref, bgate_ref, wconv_ref,
                     h_ref, pn_ref, pt_ref, k_ref, qt_ref, st_ref, dec_ref, s_ref,
                     qk_scr, gt_scr, *, tiles_per_seq):
    j = pl.program_id(0)
    tm = xm_ref.shape[0]
    n_chunks = tm // CHUNK

    @pl.when(j == 0)
    def _():
        qk_scr[...] = jnp.zeros_like(qk_scr)
        gt_scr[...] = jnp.zeros_like(gt_scr)

    srow = lax.broadcasted_iota(jnp.int32, (CHUNK, CHUNK), 0)
    slane = lax.broadcasted_iota(jnp.int32, (CHUNK, CHUNK), 1)
    tri = [(srow <= slane).astype(_BF16), (srow >= slane).astype(_BF16)]
    scans = [_gate_scan_start(gt_scr[:, c * CHUNK:(c + 1) * CHUNK], tri)
             for c in range(n_chunks)]

    n_pairs = N_MLSTM_HEADS // 2
    zero_head = jnp.zeros((HEAD_DIM, CHUNK), _BF16)

    def conv_unit(p, c):
        r0 = c * CHUNK
        qt = _conv_unit(qk_scr, wconv_ref, p, r0).T.astype(_BF16)
        k = (_conv_unit(qk_scr, wconv_ref, n_pairs + p, r0) * (HEAD_DIM ** -0.5)).astype(_BF16)
        qt_ref[p * LANES:(p + 1) * LANES, r0:r0 + CHUNK] = qt
        k_ref[r0:r0 + CHUNK, p * LANES:(p + 1) * LANES] = k
        return p, c, k, qt

    def score_unit(p, c, k, qt):
        for par in range(2):
            half = qt[par * HEAD_DIM:(par + 1) * HEAD_DIM]
            qtz = jnp.concatenate([zero_head, half] if par else [half, zero_head], axis=0)
            s_ref[c, 2 * p + par] = jnp.dot(k, qtz, preferred_element_type=_F32).astype(_BF16)

    conv_units = [functools.partial(conv_unit, p, c)
                  for c in range(n_chunks) for p in range(n_pairs)]
    scan_units = [functools.partial(_gate_scan_finish, scans[c], st_ref, dec_ref,
                                    slice(c * CHUNK, (c + 1) * CHUNK)) for c in range(n_chunks)]

    pos = jnp.minimum(j, pl.num_programs(0) - 2) % tiles_per_seq
    x = jnp.concatenate([xp_ref[...], xm_ref[...], xn_ref[...]], axis=0)
    xn = _rmsnorm(x, g1_ref[...]).astype(_BF16)
    n_slices = D_FF // MXU_DIM
    per_slice = -(-len(conv_units) // (n_slices - 1))
    convolved = []
    acc = None
    for t in range(n_slices):
        lo, hi = t * MXU_DIM, (t + 1) * MXU_DIM
        for done in convolved:
            score_unit(*done)
        g = jnp.dot(xn, wgu_ref[:, lo:hi], preferred_element_type=_F32)
        u = jnp.dot(xn, wgu_ref[:, D_FF + lo:D_FF + hi], preferred_element_type=_F32)
        convolved = [unit() for unit in conv_units[:per_slice]]
        del conv_units[:per_slice]
        if not convolved and scan_units:
            scan_units.pop(0)()
        act = (jax.nn.silu(g) * u).astype(_BF16)
        part = jnp.dot(act, wd_ref[lo:hi, :], preferred_element_type=_F32)
        acc = part if acc is None else acc + part
    for done in convolved:
        score_unit(*done)
    while scan_units:
        scan_units.pop(0)()
    h = x + 0.5 * acc
    h_ref[...] = h[HALO:HALO + tm]
    un = _rmsnorm(h, gmix_ref[...]).astype(_BF16)
    qk = jnp.dot(un, wqk_ref[...], preferred_element_type=_F32)
    qk_scr[0:HALO, :] = jnp.where(pos > 0, qk[0:HALO], 0.0)
    qk_scr[HALO:HALO + tm, :] = qk[HALO:HALO + tm]
    qk_scr[HALO + tm:, :] = jnp.where(pos < tiles_per_seq - 1, qk[HALO + tm:], 0.0)
    un_c = un[HALO:HALO + tm]
    pn_ref[...] = jnp.dot(un_c, wn_ref[...], preferred_element_type=_F32).astype(_BF16)
    t = lax.dot_general(wt_ref[...], un_c, (((1,), (1,)), ((), ())),
                        preferred_element_type=_F32)
    pt_ref[...] = t[:PT_ROWS].astype(_BF16)
    gt_scr[...] = t[PT_ROWS:PT_ROWS + N_GATES] + bgate_ref[...]


def _ffn_proj(x2, seq, g1, wgu, wd, gmix, wqk, wn, wt, bgate, wconv):
    n = x2.shape[0]
    tm = TOKEN_TILE
    nt = n // tm
    per_seq = seq // tm
    per = tm // HALO
    nhalo = n // HALO
    ta = lambda j: jnp.minimum(j, nt - 1)
    tf = lambda j: jnp.maximum(j - 1, 0)
    rowa = lambda w: pl.BlockSpec((tm, w), lambda j: (ta(j), 0))
    rowf = lambda w: pl.BlockSpec((tm, w), lambda j: (tf(j), 0))
    cola = lambda r: pl.BlockSpec((None, r, tm), lambda j: (ta(j) // per_seq, 0, ta(j) % per_seq))
    colf = lambda r: pl.BlockSpec((None, r, tm), lambda j: (tf(j) // per_seq, 0, tf(j) % per_seq))
    b = n // seq
    return pl.pallas_call(
        functools.partial(_ffn_proj_kernel, tiles_per_seq=per_seq),
        grid=(nt + 1,),
        in_specs=[rowa(D_MODEL),
                  pl.BlockSpec((HALO, D_MODEL), lambda j: (jnp.maximum(ta(j) * per - 1, 0), 0)),
                  pl.BlockSpec((HALO, D_MODEL),
                               lambda j: (jnp.minimum((ta(j) + 1) * per, nhalo - 1), 0)),
                  _const_spec((1, D_MODEL)), _const_spec((D_MODEL, 2 * D_FF)),
                  _const_spec((D_FF, D_MODEL)), _const_spec((1, D_MODEL)),
                  _const_spec((D_MODEL, 2 * MLSTM_WIDTH)),
                  _const_spec((D_MODEL, PN_WIDTH)),
                  _const_spec((PT_ROWS + GATE_ROWS_PAD, D_MODEL)),
                  _const_spec((N_GATES, 1)), _const_spec((SUBLANES, 2 * MLSTM_WIDTH))],
        out_specs=[rowa(D_MODEL), rowa(PN_WIDTH), cola(PT_ROWS),
                   rowf(MLSTM_WIDTH), colf(MLSTM_WIDTH), colf(2 * ST_ROWS), rowf(LANES),
                   pl.BlockSpec((tm // CHUNK, N_MLSTM_HEADS, CHUNK, CHUNK),
                                lambda j: (tf(j), 0, 0, 0))],
        out_shape=[jax.ShapeDtypeStruct((n, D_MODEL), _F32),
                   jax.ShapeDtypeStruct((n, PN_WIDTH), _BF16),
                   jax.ShapeDtypeStruct((b, PT_ROWS, seq), _BF16),
                   jax.ShapeDtypeStruct((n, MLSTM_WIDTH), _BF16),
                   jax.ShapeDtypeStruct((b, MLSTM_WIDTH, seq), _BF16),
                   jax.ShapeDtypeStruct((b, 2 * ST_ROWS, seq), _F32),
                   jax.ShapeDtypeStruct((n, LANES), _BF16),
                   jax.ShapeDtypeStruct((n // CHUNK, N_MLSTM_HEADS, CHUNK, CHUNK), _BF16)],
        scratch_shapes=[pltpu.VMEM((tm + 2 * HALO, 2 * MLSTM_WIDTH), _F32),
                        pltpu.VMEM((N_GATES, tm), _F32)],
        compiler_params=pltpu.CompilerParams(dimension_semantics=("arbitrary",),
                                             vmem_limit_bytes=VMEM_LIMIT),
        name="ffn1_proj_prep",
    )(x2, x2, x2, g1, wgu, wd, gmix, wqk, wn, wt, bgate, wconv)


def _attn_scores(qt_ref, kcat, blk, h):
    zeros = jnp.zeros((HEAD_DIM, BLOCK), _BF16)
    keys = kcat[blk * BLOCK:(blk + 3) * BLOCK, :]
    qt = (qt_ref[h * HEAD_DIM:(h + 1) * HEAD_DIM, blk * BLOCK:(blk + 1) * BLOCK]
          * (HEAD_DIM ** -0.5))
    qtz = jnp.concatenate([qt, zeros] if h < GQA_GROUP else [zeros, qt], axis=0)
    return jnp.dot(keys, qtz, preferred_element_type=_F32)


def _attn_softmax(s, sink_ref, bias_ref, h, edge_ok):
    s = s + bias_ref[h]
    if edge_ok is not None:
        s = jnp.where(edge_ok, s, NEG)
    sk = sink_ref[h]
    m = jnp.maximum(jnp.max(s, axis=0, keepdims=True), sk)
    p = jnp.exp(s - m)
    den = jnp.sum(p, axis=0, keepdims=True) + jnp.exp(sk - m)
    return p.astype(_BF16), 1.0 / den


def _attn_values(p, rden, vcat, blk, h):
    g = h // GQA_GROUP
    vtg = vcat[g * HEAD_DIM:(g + 1) * HEAD_DIM, blk * BLOCK:(blk + 3) * BLOCK]
    return jnp.dot(vtg, p, preferred_element_type=_F32) * rden


def _mix_ffn_kernel(sink_ref, qt_ref, kp_ref, kc_ref, kn_ref, vp_ref, vc_ref, vn_ref, bias_ref,
                    h_ref, hf_ref, hb_ref, o_ref, gout_ref, wout_ref, g2_ref, wgu_ref, wd_ref,
                    gf_ref, y_ref, attn_ref, *, tiles_per_seq):
    j = pl.program_id(0)
    tm = h_ref.shape[0]
    nblk = tm // BLOCK

    @pl.when(j == 0)
    def _():
        attn_ref[...] = jnp.zeros_like(attn_ref)

    pos = jnp.minimum(j, pl.num_programs(0) - 2) % tiles_per_seq
    kcat = jnp.concatenate([kp_ref[...], kc_ref[...], kn_ref[...]], axis=0)
    vcat = jnp.concatenate([vp_ref[...], vc_ref[...], vn_ref[...]], axis=1)
    krow = lax.broadcasted_iota(jnp.int32, (3 * BLOCK, BLOCK), 0)
    edge = {0: (krow >= BLOCK) | (pos > 0)}
    ok_next = (krow < 2 * BLOCK) | (pos < tiles_per_seq - 1)
    edge[nblk - 1] = ok_next if nblk > 1 else edge[0] & ok_next
    units = [(blk, h) for blk in range(nblk) for h in range(N_ATTN_HEADS)]
    n_slices = D_FF // MXU_DIM
    per = -(-len(units) // n_slices)
    groups = [units[i * per:(i + 1) * per] for i in range(n_slices)]
    attn_out = {}
    scored = [_attn_scores(qt_ref, kcat, blk, hh) for blk, hh in groups[0]]
    weighted = []

    lane = lax.broadcasted_iota(jnp.int32, (tm, LANES), 1)
    lo_half = lane < HEAD_DIM
    hm_tiles = []
    for p in range(N_MLSTM_HEADS // 2):
        sl = slice(p * LANES, (p + 1) * LANES)
        hm = (jax.nn.sigmoid(o_ref[:, sl].astype(_F32))
              * (hf_ref[:, sl].astype(_F32) + hb_ref[:, sl].astype(_F32)))
        sq = hm * hm
        ss_lo = jnp.sum(jnp.where(lo_half, sq, 0.0), axis=1, keepdims=True)
        ss_hi = jnp.sum(jnp.where(lo_half, 0.0, sq), axis=1, keepdims=True)
        ms = jnp.where(lo_half, ss_lo, ss_hi) * (1.0 / HEAD_DIM)
        hm_tiles.append((hm * lax.rsqrt(ms + EPS) * gout_ref[:, sl]).astype(_BF16))
    hm_all = jnp.concatenate(hm_tiles, axis=1)
    h = (h_ref[...]
         + jnp.dot(attn_ref[...], wout_ref[0:ATTN_WIDTH, :], preferred_element_type=_F32)
         + jnp.dot(hm_all, wout_ref[ATTN_WIDTH:MIX_WIDTH, :], preferred_element_type=_F32))
    hn = _rmsnorm(h, g2_ref[...]).astype(_BF16)
    acc = None
    for t in range(n_slices):
        lo, hi = t * MXU_DIM, (t + 1) * MXU_DIM
        upcoming = ([_attn_scores(qt_ref, kcat, blk, hh) for blk, hh in groups[t + 1]]
                    if t + 1 < n_slices else [])
        g = jnp.dot(hn, wgu_ref[:, lo:hi], preferred_element_type=_F32)
        u = jnp.dot(hn, wgu_ref[:, D_FF + lo:D_FF + hi], preferred_element_type=_F32)
        for (blk, hh), p, rden in weighted:
            attn_out[blk, hh] = _attn_values(p, rden, vcat, blk, hh)
        weighted = [((blk, hh),) + _attn_softmax(s, sink_ref, bias_ref, hh, edge.get(blk))
                    for (blk, hh), s in zip(groups[t], scored)]
        act = (jax.nn.silu(g) * u).astype(_BF16)
        part = jnp.dot(act, wd_ref[lo:hi, :], preferred_element_type=_F32)
        acc = part if acc is None else acc + part
        scored = upcoming
    for (blk, hh), p, rden in weighted:
        attn_out[blk, hh] = _attn_values(p, rden, vcat, blk, hh)
    h = h + 0.5 * acc
    y_ref[...] = _rmsnorm(h, gf_ref[...])

    for blk in range(nblk):
        heads = [attn_out[blk, hh] for hh in range(N_ATTN_HEADS)]
        attn_ref[blk * BLOCK:(blk + 1) * BLOCK, :] = (
            jnp.concatenate(heads, axis=0).T.astype(_BF16))


def _mix_ffn(seq, pn2, pt3, h2, hf2, hb2, sink, bias_t, gout, wout, g2, wgu, wd, gf):
    n = h2.shape[0]
    tm = TOKEN_TILE
    nt = n // tm
    per_seq = seq // tm
    per_blk = tm // BLOCK
    nblocks = n // BLOCK
    blk_per_seq = seq // BLOCK
    ta = lambda j: jnp.minimum(j, nt - 1)
    tf = lambda j: jnp.maximum(j - 1, 0)
    kcol = PN_K_A // KV_WIDTH
    vrow = PT_V_A // KV_WIDTH
    kprev = lambda j: jnp.maximum(ta(j) * per_blk - 1, 0)
    knext = lambda j: jnp.minimum((ta(j) + 1) * per_blk, nblocks - 1)
    vmain = lambda j: (ta(j) // per_seq, vrow, ta(j) % per_seq)
    vprev = lambda j: (kprev(j) // blk_per_seq, vrow, kprev(j) % blk_per_seq)
    vnext = lambda j: (knext(j) // blk_per_seq, vrow, knext(j) % blk_per_seq)
    row = lambda w, c=0: pl.BlockSpec((tm, w), lambda j: (tf(j), c))
    return pl.pallas_call(
        functools.partial(_mix_ffn_kernel, tiles_per_seq=per_seq),
        grid=(nt + 1,),
        in_specs=[pl.BlockSpec(memory_space=pltpu.SMEM),
                  pl.BlockSpec((None, ATTN_WIDTH, tm),
                               lambda j: (ta(j) // per_seq, PT_Q_A // ATTN_WIDTH, ta(j) % per_seq)),
                  pl.BlockSpec((BLOCK, KV_WIDTH), lambda j: (kprev(j), kcol)),
                  pl.BlockSpec((tm, KV_WIDTH), lambda j: (ta(j), kcol)),
                  pl.BlockSpec((BLOCK, KV_WIDTH), lambda j: (knext(j), kcol)),
                  pl.BlockSpec((None, KV_WIDTH, BLOCK), vprev),
                  pl.BlockSpec((None, KV_WIDTH, tm), vmain),
                  pl.BlockSpec((None, KV_WIDTH, BLOCK), vnext),
                  _const_spec((N_ATTN_HEADS, 3 * BLOCK, BLOCK)),
                  row(D_MODEL), row(MLSTM_WIDTH), row(MLSTM_WIDTH),
                  row(MLSTM_WIDTH, PN_O_M // MLSTM_WIDTH),
                  _const_spec((1, MLSTM_WIDTH)), _const_spec((MIX_WIDTH, D_MODEL)),
                  _const_spec((1, D_MODEL)), _const_spec((D_MODEL, 2 * D_FF)),
                  _const_spec((D_FF, D_MODEL)), _const_spec((1, D_MODEL))],
        out_specs=row(D_MODEL),
        out_shape=jax.ShapeDtypeStruct((n, D_MODEL), _F32),
        scratch_shapes=[pltpu.VMEM((tm, ATTN_WIDTH), _BF16)],
        compiler_params=pltpu.CompilerParams(dimension_semantics=("arbitrary",),
                                             vmem_limit_bytes=VMEM_LIMIT),
        name="attn_out_ffn2",
    )(sink, pt3, pn2, pn2, pn2, pt3, pt3, pt3, bias_t, h2, hf2, hb2, pn2,
      gout, wout, g2, wgu, wd, gf)


def _chunk_rows(st, m_prev):
    nh = N_MLSTM_HEADS
    a = st[ST_A:ST_A + nh]
    gl = st[ST_GL:ST_GL + nh]
    a_end = st[ST_AEND:ST_AEND + nh]
    m_loc = st[ST_MLOC:ST_MLOC + nh]
    g = jnp.maximum(m_prev, gl)
    m_new = jnp.maximum(a_end + m_prev, m_loc)
    return dict(e=st[ST_E:ST_E + nh],
                inter=jnp.exp(m_prev - g),
                floor=jnp.exp(-(a + g)),
                neg_g=_split3(-g), m_new=m_new,
                s_old=jnp.exp(a_end + m_prev - m_new), s_loc=jnp.exp(m_loc - m_new))


def _chunk_input_matmuls(d, rows, k, qt, vt, dec):
    nh = N_MLSTM_HEADS
    row8 = lax.broadcasted_iota(jnp.int32, (nh, CHUNK), 0)
    zero8 = jnp.zeros((nh, CHUNK), _F32)
    zero_half = jnp.zeros((DEC_K, CHUNK), _F32)
    zero_head = jnp.zeros((HEAD_DIM, CHUNK), _BF16)
    ng_hi, ng_mid, ng_lo = rows["neg_g"]
    e = rows["e"]
    heads = []
    for h in range(nh):
        p, par = divmod(h, 2)
        hs = slice(h * HEAD_DIM, (h + 1) * HEAD_DIM)
        qt_h = qt[hs, :]
        qtz = jnp.concatenate([zero_head, qt_h] if par else [qt_h, zero_head], axis=0)
        hot = row8 == h
        one = jnp.where(hot, 1.0, 0.0)
        blk = jnp.concatenate(
            [one, one, one, jnp.where(hot, ng_hi, 0.0), jnp.where(hot, ng_mid, 0.0),
             jnp.where(hot, ng_lo, 0.0), zero8, zero8], axis=0)
        x = jnp.concatenate([zero_half, blk] if d else [blk, zero_half], axis=0)
        k_pair = k[:, p * LANES:(p + 1) * LANES]
        vt_h = vt[hs, :]
        heads.append(dict(
            qtz=qtz, vt=vt_h,
            dexp=jnp.dot(dec, x.astype(_BF16), preferred_element_type=_F32),
            c_loc=jnp.dot((vt_h.astype(_F32) * e[h:h + 1, :]).astype(_BF16), k_pair,
                          preferred_element_type=_F32)))
    n_loc = jnp.dot(e.astype(_BF16), k, preferred_element_type=_F32)
    return heads, n_loc


def _chunk_state_matmuls(heads, qt, c_state, n_state):
    for hd, cz in zip(heads, c_state):
        hd["nd_state"] = jnp.dot(cz.astype(_BF16), hd["qtz"], preferred_element_type=_F32)
    return jnp.dot(n_state.astype(_BF16), qt, preferred_element_type=_F32)


def _state_update(rows, heads, n_loc, c_state, n_state):
    nh = N_MLSTM_HEADS
    s_old, s_loc = rows["s_old"], rows["s_loc"]
    c_new = [s_old[h:h + 1, :] * cz + s_loc[h:h + 1, :] * hd["c_loc"]
             for h, (hd, cz) in enumerate(zip(heads, c_state))]
    rep = MLSTM_WIDTH // LANES
    head_of_lane = lax.broadcasted_iota(jnp.int32, (nh, MLSTM_WIDTH), 1) // HEAD_DIM
    own = head_of_lane == lax.broadcasted_iota(jnp.int32, (nh, MLSTM_WIDTH), 0)
    n_new = (jnp.concatenate([s_old] * rep, axis=1) * n_state
             + jnp.concatenate([s_loc] * rep, axis=1) * jnp.where(own, n_loc, 0.0))
    return c_new, n_new


def _chunk_weighted_values(d, heads, s_ref, i):
    row = lax.broadcasted_iota(jnp.int32, (CHUNK, CHUNK), 0)
    lane = lax.broadcasted_iota(jnp.int32, (CHUNK, CHUNK), 1)
    keep = (row >= lane) if d else (row <= lane)
    for h, hd in enumerate(heads):
        qk = s_ref[i, h].astype(_F32) * jnp.exp(jnp.where(keep, hd["dexp"], NEG))
        hd["den_intra"] = jnp.sum(qk, axis=0, keepdims=True)
        hd["nd_intra"] = jnp.dot(hd["vt"], qk.astype(_BF16), preferred_element_type=_F32)


def _chunk_finish(rows, heads, den_state, out_ref, rs):
    inter, floor = rows["inter"], rows["floor"]
    for p in range(N_MLSTM_HEADS // 2):
        tiles = []
        for h in (2 * p, 2 * p + 1):
            hd = heads[h]
            it = inter[h:h + 1, :]
            den = hd["den_intra"] + it * den_state[h:h + 1, :]
            num = hd["nd_intra"] + it * hd["nd_state"]
            tiles.append(num / jnp.maximum(jnp.abs(den), floor[h:h + 1, :]))
        out_ref[rs, p * LANES:(p + 1) * LANES] = jnp.concatenate(tiles, axis=0).T.astype(_BF16)


def _mlstm_seq_kernel(kf_ref, qtf_ref, vtf_ref, decf_ref, stf_ref, sf_ref,
                      kb_ref, qtb_ref, vtb_ref, decb_ref, stb_ref, sb_ref,
                      hf_ref, hb_ref, c_ref, n_ref, m_ref):
    @pl.when(pl.program_id(1) == 0)
    def _():
        c_ref[...] = jnp.zeros_like(c_ref)
        n_ref[...] = jnp.zeros_like(n_ref)
        m_ref[...] = jnp.full_like(m_ref, NEG)

    nh = N_MLSTM_HEADS
    n_sub = kf_ref.shape[0] // CHUNK
    refs = [(kf_ref, qtf_ref, vtf_ref, decf_ref, stf_ref, hf_ref),
            (kb_ref, qtb_ref, vtb_ref, decb_ref, stb_ref, hb_ref)]
    score_refs = [sf_ref, sb_ref]
    order = [list(range(n_sub)), list(range(n_sub - 1, -1, -1))]
    c_state = [[c_ref[d, h] for h in range(nh)] for d in range(2)]
    n_state = [n_ref[d] for d in range(2)]
    rows = [[], []]
    for d in range(2):
        m = m_ref[d]
        for i in order[d]:
            rows[d].append(_chunk_rows(refs[d][4][:, i * CHUNK:(i + 1) * CHUNK], m))
            m = rows[d][-1]["m_new"]
        m_ref[d] = m

    work = [[None] * n_sub, [None] * n_sub]

    def start(step):
        for d in range(2):
            k_ref, qt_ref, vt_ref, dec_ref, _, _ = refs[d]
            ts = slice(order[d][step] * CHUNK, (order[d][step] + 1) * CHUNK)
            heads, n_loc = _chunk_input_matmuls(d, rows[d][step], k_ref[ts, :], qt_ref[:, ts],
                                                vt_ref[:, ts], dec_ref[ts, :])
            work[d][step] = dict(heads=heads, n_loc=n_loc, ts=ts)
        for d in range(2):
            w = work[d][step]
            w["den_state"] = _chunk_state_matmuls(w["heads"], refs[d][1][:, w["ts"]],
                                                  c_state[d], n_state[d])
        for d in range(2):
            w = work[d][step]
            c_state[d], n_state[d] = _state_update(rows[d][step], w["heads"], w["n_loc"],
                                                   c_state[d], n_state[d])

    def weigh(step):
        for d in range(2):
            _chunk_weighted_values(d, work[d][step]["heads"], score_refs[d], order[d][step])

    def finish(step):
        for d in range(2):
            w = work[d][step]
            _chunk_finish(rows[d][step], w["heads"], w["den_state"], refs[d][5], w["ts"])

    for step in range(n_sub + 2):
        if step < n_sub:
            start(step)
        if 1 <= step <= n_sub:
            weigh(step - 1)
        if step >= 2:
            finish(step - 2)
    for d in range(2):
        for h in range(nh):
            c_ref[d, h] = c_state[d][h]
        n_ref[d] = n_state[d]


def _mlstm_seq(k3, qt3, pt3, dec3, st3, sc5):
    b, s, _ = k3.shape
    sub = SEQ_CHUNKS if (s // CHUNK) % SEQ_CHUNKS == 0 else 1
    blk = sub * CHUNK
    nc = s // blk
    vrow = PT_V_M // MLSTM_WIDTH

    def specs(pos, d):
        return [pl.BlockSpec((None, blk, MLSTM_WIDTH), lambda bi, c: (bi, pos(c), 0)),
                pl.BlockSpec((None, MLSTM_WIDTH, blk), lambda bi, c: (bi, 0, pos(c))),
                pl.BlockSpec((None, MLSTM_WIDTH, blk), lambda bi, c: (bi, vrow, pos(c))),
                pl.BlockSpec((None, blk, LANES), lambda bi, c: (bi, pos(c), 0)),
                pl.BlockSpec((None, ST_ROWS, blk), lambda bi, c: (bi, d, pos(c))),
                pl.BlockSpec((None, sub, N_MLSTM_HEADS, CHUNK, CHUNK),
                             lambda bi, c: (bi, pos(c), 0, 0, 0))]

    fwd = lambda c: c
    bwd = lambda c: nc - 1 - c
    out = lambda pos: pl.BlockSpec((None, blk, MLSTM_WIDTH), lambda bi, c: (bi, pos(c), 0))
    args = (k3, qt3, pt3, dec3, st3, sc5)
    return pl.pallas_call(
        _mlstm_seq_kernel,
        grid=(b, nc),
        in_specs=specs(fwd, 0) + specs(bwd, 1),
        out_specs=[out(fwd), out(bwd)],
        out_shape=[jax.ShapeDtypeStruct((b, s, MLSTM_WIDTH), _BF16)] * 2,
        scratch_shapes=[pltpu.VMEM((2, N_MLSTM_HEADS, HEAD_DIM, LANES), _F32),
                        pltpu.VMEM((2, N_MLSTM_HEADS, MLSTM_WIDTH), _F32),
                        pltpu.VMEM((2, N_MLSTM_HEADS, LANES), _F32)],
        compiler_params=pltpu.CompilerParams(dimension_semantics=("parallel", "arbitrary")),
        name="mlstm_seq",
    )(*args, *args)


def _t5_bucket(rel):
    nb = N_BUCKETS // 2
    ret = (rel > 0).astype(np.int32) * nb
    n = np.abs(rel)
    max_exact = nb // 2
    large = max_exact + (np.log(np.maximum(n, 1) / max_exact)
                         / math.log(MAX_DISTANCE / max_exact) * (nb - max_exact)).astype(np.int32)
    large = np.minimum(large, nb - 1)
    return (ret + np.where(n < max_exact, n, large)).astype(np.int32)


def _prepare_params(g_ffn1, w_ffn1_gu, w_ffn1_down, g_mix, w_in, w_conv, b_gates, attn_sink,
                    g_mlstm_out, w_out, g_ffn2, w_ffn2_gu, w_ffn2_down, rel_table, g_final):
    row = lambda g: g.reshape(1, -1).astype(_F32)
    offs = np.cumsum((0,) + SPLIT_SIZES)
    q_a, k_a, v_a, q_m, k_m, v_m, o_m, gate = (w_in[0][:, offs[i]:offs[i + 1]] for i in range(8))
    wqk = jnp.concatenate([q_m, k_m], axis=1).astype(_BF16)
    wn = jnp.concatenate([o_m, k_a], axis=1).astype(_BF16)
    gate_pad = jnp.pad(gate, ((0, 0), (0, GATE_ROWS_PAD - N_GATES)))
    wt = jnp.concatenate([q_a, v_m, v_a, gate_pad], axis=1).T.astype(_BF16)
    bgate = b_gates[0].reshape(N_GATES, 1).astype(_F32)
    wconv = jnp.pad(w_conv[0].astype(_F32), ((0, SUBLANES - CONV_WIDTH), (0, 0)))
    kj = np.arange(3 * BLOCK)[:, None]
    qi = np.arange(BLOCK)[None, :]
    bucket = jnp.asarray(_t5_bucket((kj - BLOCK) - qi).reshape(-1))
    onehot = (bucket[None, :] == jnp.arange(N_BUCKETS)[:, None]).astype(_F32)
    bias_t = jnp.dot(rel_table.astype(_F32).T, onehot, precision=lax.Precision.HIGHEST)
    bias_t = bias_t.reshape(N_ATTN_HEADS, 3 * BLOCK, BLOCK)
    window = np.abs((kj - BLOCK) - qi) <= WINDOW
    bias_t = jnp.where(jnp.asarray(window)[None], bias_t, NEG)
    return dict(
        g1=row(g_ffn1[0]), wgu1=w_ffn1_gu[0].astype(_BF16), wd1=w_ffn1_down[0].astype(_BF16),
        gmix=row(g_mix[0]), wqk=wqk, wn=wn, wt=wt, bgate=bgate, wconv=wconv,
        sink=attn_sink[0].astype(_F32), bias_t=bias_t, gout=row(g_mlstm_out[0]),
        wout=w_out[0].astype(_BF16), g2=row(g_ffn2[0]), wgu2=w_ffn2_gu[0].astype(_BF16),
        wd2=w_ffn2_down[0].astype(_BF16), gf=row(g_final))


def _trunk(x, p):
    b, s, d = x.shape
    n = b * s
    h2, pn2, pt3, k2, qt3, st3, dec2, sc4 = _ffn_proj(
        x.reshape(n, d), s, p["g1"], p["wgu1"], p["wd1"], p["gmix"], p["wqk"], p["wn"], p["wt"],
        p["bgate"], p["wconv"])
    hf3, hb3 = _mlstm_seq(k2.reshape(b, s, MLSTM_WIDTH), qt3, pt3, dec2.reshape(b, s, LANES), st3,
                          sc4.reshape(b, s // CHUNK, N_MLSTM_HEADS, CHUNK, CHUNK))
    y2 = _mix_ffn(s, pn2, pt3, h2, hf3.reshape(n, MLSTM_WIDTH), hb3.reshape(n, MLSTM_WIDTH),
                  p["sink"], p["bias_t"], p["gout"], p["wout"], p["g2"], p["wgu2"], p["wd2"],
                  p["gf"])
    return y2.reshape(b, s, d)


def kernel(x_prompt, x_sample, g_ffn1, w_ffn1_gu, w_ffn1_down, g_mix, w_in, w_conv, b_gates,
           attn_sink, g_mlstm_out, w_out, g_ffn2, w_ffn2_gu, w_ffn2_down, rel_bias_table, g_final):
    p = _prepare_params(g_ffn1, w_ffn1_gu, w_ffn1_down, g_mix, w_in, w_conv, b_gates, attn_sink,
                        g_mlstm_out, w_out, g_ffn2, w_ffn2_gu, w_ffn2_down, rel_bias_table, g_final)
    return (_trunk(x_prompt, p), _trunk(x_sample, p))
```

```python
import functools
import math

import jax
import jax.numpy as jnp
import numpy as np
from jax import lax
from jax.experimental import pallas as pl
from jax.experimental.pallas import tpu as pltpu

D_MODEL = 1024
HEAD_DIM = 64
N_ATTN_HEADS = 8
N_KV_HEADS = 2
GQA_GROUP = N_ATTN_HEADS // N_KV_HEADS
ATTN_WIDTH = N_ATTN_HEADS * HEAD_DIM
KV_WIDTH = N_KV_HEADS * HEAD_DIM
N_MLSTM_HEADS = 8
MLSTM_WIDTH = N_MLSTM_HEADS * HEAD_DIM
MIX_WIDTH = ATTN_WIDTH + MLSTM_WIDTH
WINDOW = 128
BLOCK = 128
N_BUCKETS = 32
MAX_DISTANCE = 128
CHUNK = 128
CONV_WIDTH = 5
D_FF = 2816
EPS = 1e-6
NEG = -1e30
LOG2E = math.log2(math.e)
SPLIT_SIZES = (ATTN_WIDTH, KV_WIDTH, KV_WIDTH, MLSTM_WIDTH, MLSTM_WIDTH, MLSTM_WIDTH,
               MLSTM_WIDTH, 4 * N_MLSTM_HEADS)
N_GATES = 4 * N_MLSTM_HEADS

LANES = 128
SUBLANES = 8
BF16_SUBLANES = 16
MXU_DIM = 256

PN_O_M = 0
PN_K_A = PN_O_M + MLSTM_WIDTH
PN_WIDTH = PN_K_A + KV_WIDTH
PT_Q_A = 0
PT_V_M = ATTN_WIDTH
PT_V_A = PT_V_M + MLSTM_WIDTH
PT_ROWS = PT_V_A + KV_WIDTH
GATE_ROWS_PAD = LANES

HALO = BF16_SUBLANES
TOKEN_TILE = 512
SEQ_CHUNKS = 8
FF_SLICE = MXU_DIM
FF_SLICES = tuple((lo, min(lo + FF_SLICE, D_FF)) for lo in range(0, D_FF, FF_SLICE))
VMEM_LIMIT = 56 * 1024 * 1024

ST_A, ST_GL, ST_E, ST_AEND, ST_MLOC = (i * N_MLSTM_HEADS for i in range(5))
ST_ROWS = 5 * N_MLSTM_HEADS
DEC_K = LANES // 2

_F32 = jnp.float32
_BF16 = jnp.bfloat16


def _rmsnorm(x, g):
    y = x * lax.rsqrt(jnp.mean(x * x, axis=-1, keepdims=True) + EPS)
    return y * g


def _split3(x):
    hi = x.astype(_BF16).astype(_F32)
    rem = x - hi
    mid = rem.astype(_BF16).astype(_F32)
    lo = (rem - mid).astype(_BF16).astype(_F32)
    return hi, mid, lo


def _const_spec(shape):
    nd = len(shape)
    return pl.BlockSpec(shape, lambda *_: (0,) * nd, pipeline_mode=pl.Buffered(1))


def _conv_unit(qk_ref, wconv_ref, c, r0):
    cs = slice(c * LANES, (c + 1) * LANES)
    w = wconv_ref[:, cs]
    y = None
    for tap in range(CONV_WIDTH):
        lo = r0 + HALO - CONV_WIDTH // 2 + tap
        term = qk_ref[lo:lo + CHUNK, cs] * w[tap:tap + 1, :]
        y = term if y is None else y + term
    return jax.nn.silu(y)


def _gate_scan_start(gt, tri):
    nh = N_MLSTM_HEADS
    out = []
    for d in range(2):
        ig = gt[2 * d * nh:(2 * d + 1) * nh, :]
        fg = gt[(2 * d + 1) * nh:(2 * d + 2) * nh, :]
        lf = jnp.minimum(fg, 0.0) - jnp.log1p(jnp.exp(-jnp.abs(fg)))
        parts = jnp.concatenate(_split3(lf), axis=0).astype(_BF16)
        sums = jnp.dot(parts, tri[d], preferred_element_type=_F32)
        out.append((ig, sums[0:nh] + sums[nh:2 * nh] + sums[2 * nh:3 * nh]))
    return out


def _gate_scan_finish(scans, st_ref, dec_ref, cs):
    nh = N_MLSTM_HEADS
    lane8 = lax.broadcasted_iota(jnp.int32, (nh, CHUNK), 1)
    ones = jnp.ones((3 * nh, CHUNK), _F32)
    zpad = jnp.zeros((DEC_K - 6 * nh, CHUNK), _F32)
    dec_rows = []
    for d, (ig, a) in enumerate(scans):
        r = ig - a
        gl = r
        sh = 1
        while sh < CHUNK:
            if d == 0:
                gl = jnp.maximum(gl, jnp.where(lane8 >= sh, pltpu.roll(gl, sh, 1), NEG))
            else:
                gl = jnp.maximum(
                    gl, jnp.where(lane8 < CHUNK - sh, pltpu.roll(gl, CHUNK - sh, 1), NEG))
            sh *= 2
        far = CHUNK - 1 if d == 0 else 0
        a_end = jnp.broadcast_to(a[:, far:far + 1], (nh, CHUNK))
        m_loc = a_end + jnp.broadcast_to(gl[:, far:far + 1], (nh, CHUNK))
        e = jnp.exp(a_end + r - m_loc)
        st_ref[d * ST_ROWS:(d + 1) * ST_ROWS, cs] = jnp.concatenate(
            [a, gl, e, a_end, m_loc], axis=0)
        dec_rows += list(_split3(r * LOG2E)) + [ones, zpad]
    dec_ref[cs, :] = jnp.concatenate(dec_rows, axis=0).T.astype(_BF16)


def _ffn_proj_kernel(xm_ref, xp_ref, xn_ref, g1_ref, wgu_ref, wd_ref, gmix_ref, wqk_ref, wn_ref,
                     wt_ref, bgate_ref, wconv_ref,
                     h_ref, pn_ref, pt_ref, k_ref, qt_ref, st_ref, dec_ref, s_ref,
                     qk_scr, gt_scr, *, tiles_per_seq):
    j = pl.program_id(0)
    tm = xm_ref.shape[0]
    n_chunks = tm // CHUNK

    @pl.when(j == 0)
    def _():
        qk_scr[...] = jnp.zeros_like(qk_scr)
        gt_scr[...] = jnp.zeros_like(gt_scr)

    srow = lax.broadcasted_iota(jnp.int32, (CHUNK, CHUNK), 0)
    slane = lax.broadcasted_iota(jnp.int32, (CHUNK, CHUNK), 1)
    tri = [(srow <= slane).astype(_BF16), (srow >= slane).astype(_BF16)]
    scans = [_gate_scan_start(gt_scr[:, c * CHUNK:(c + 1) * CHUNK], tri)
             for c in range(n_chunks)]

    n_pairs = N_MLSTM_HEADS // 2
    zero_head = jnp.zeros((HEAD_DIM, CHUNK), _BF16)

    def conv_unit(p, c):
        r0 = c * CHUNK
        qt = _conv_unit(qk_scr, wconv_ref, p, r0).T.astype(_BF16)
        k = (_conv_unit(qk_scr, wconv_ref, n_pairs + p, r0) * (HEAD_DIM ** -0.5)).astype(_BF16)
        qt_ref[p * LANES:(p + 1) * LANES, r0:r0 + CHUNK] = qt
        k_ref[r0:r0 + CHUNK, p * LANES:(p + 1) * LANES] = k
        return p, c, k, qt

    def score_unit(p, c, k, qt):
        for par in range(2):
            half = qt[par * HEAD_DIM:(par + 1) * HEAD_DIM]
            qtz = jnp.concatenate([zero_head, half] if par else [half, zero_head], axis=0)
            s_ref[c, 2 * p + par] = jnp.dot(k, qtz, preferred_element_type=_F32).astype(_BF16)

    conv_units = [functools.partial(conv_unit, p, c)
                  for c in range(n_chunks) for p in range(n_pairs)]
    scan_units = [functools.partial(_gate_scan_finish, scans[c], st_ref, dec_ref,
                                    slice(c * CHUNK, (c + 1) * CHUNK)) for c in range(n_chunks)]

    pos = jnp.minimum(j, pl.num_programs(0) - 2) % tiles_per_seq
    x = jnp.concatenate([xp_ref[...], xm_ref[...], xn_ref[...]], axis=0)
    xn = _rmsnorm(x, g1_ref[...]).astype(_BF16)
    per_slice = -(-len(conv_units) // (len(FF_SLICES) - 1))
    convolved = []
    acc = None
    for lo, hi in FF_SLICES:
        for done in convolved:
            score_unit(*done)
        g = jnp.dot(xn, wgu_ref[:, lo:hi], preferred_element_type=_F32)
        u = jnp.dot(xn, wgu_ref[:, D_FF + lo:D_FF + hi], preferred_element_type=_F32)
        convolved = [unit() for unit in conv_units[:per_slice]]
        del conv_units[:per_slice]
        if not convolved and scan_units:
            scan_units.pop(0)()
        act = (jax.nn.silu(g) * u).astype(_BF16)
        part = jnp.dot(act, wd_ref[lo:hi, :], preferred_element_type=_F32)
        acc = part if acc is None else acc + part
    for done in convolved:
        score_unit(*done)
    while scan_units:
        scan_units.pop(0)()
    h = x + 0.5 * acc
    h_ref[...] = h[HALO:HALO + tm]
    un = _rmsnorm(h, gmix_ref[...]).astype(_BF16)
    qk = jnp.dot(un, wqk_ref[...], preferred_element_type=_F32)
    qk_scr[0:HALO, :] = jnp.where(pos > 0, qk[0:HALO], 0.0)
    qk_scr[HALO:HALO + tm, :] = qk[HALO:HALO + tm]
    qk_scr[HALO + tm:, :] = jnp.where(pos < tiles_per_seq - 1, qk[HALO + tm:], 0.0)
    un_c = un[HALO:HALO + tm]
    pn_ref[...] = jnp.dot(un_c, wn_ref[...], preferred_element_type=_F32).astype(_BF16)
    t = lax.dot_general(wt_ref[...], un_c, (((1,), (1,)), ((), ())),
                        preferred_element_type=_F32)
    pt_ref[...] = t[:PT_ROWS].astype(_BF16)
    gt_scr[...] = t[PT_ROWS:PT_ROWS + N_GATES] + bgate_ref[...]


def _ffn_proj(x2, seq, g1, wgu, wd, gmix, wqk, wn, wt, bgate, wconv):
    n = x2.shape[0]
    tm = TOKEN_TILE
    nt = n // tm
    per_seq = seq // tm
    per = tm // HALO
    nhalo = n // HALO
    ta = lambda j: jnp.minimum(j, nt - 1)
    tf = lambda j: jnp.maximum(j - 1, 0)
    rowa = lambda w: pl.BlockSpec((tm, w), lambda j: (ta(j), 0))
    rowf = lambda w: pl.BlockSpec((tm, w), lambda j: (tf(j), 0))
    cola = lambda r: pl.BlockSpec((None, r, tm), lambda j: (ta(j) // per_seq, 0, ta(j) % per_seq))
    colf = lambda r: pl.BlockSpec((None, r, tm), lambda j: (tf(j) // per_seq, 0, tf(j) % per_seq))
    b = n // seq
    return pl.pallas_call(
        functools.partial(_ffn_proj_kernel, tiles_per_seq=per_seq),
        grid=(nt + 1,),
        in_specs=[rowa(D_MODEL),
                  pl.BlockSpec((HALO, D_MODEL), lambda j: (jnp.maximum(ta(j) * per - 1, 0), 0)),
                  pl.BlockSpec((HALO, D_MODEL),
                               lambda j: (jnp.minimum((ta(j) + 1) * per, nhalo - 1), 0)),
                  _const_spec((1, D_MODEL)), _const_spec((D_MODEL, 2 * D_FF)),
                  _const_spec((D_FF, D_MODEL)), _const_spec((1, D_MODEL)),
                  _const_spec((D_MODEL, 2 * MLSTM_WIDTH)),
                  _const_spec((D_MODEL, PN_WIDTH)),
                  _const_spec((PT_ROWS + GATE_ROWS_PAD, D_MODEL)),
                  _const_spec((N_GATES, 1)), _const_spec((SUBLANES, 2 * MLSTM_WIDTH))],
        out_specs=[rowa(D_MODEL), rowa(PN_WIDTH), cola(PT_ROWS),
                   rowf(MLSTM_WIDTH), colf(MLSTM_WIDTH), colf(2 * ST_ROWS), rowf(LANES),
                   pl.BlockSpec((tm // CHUNK, N_MLSTM_HEADS, CHUNK, CHUNK),
                                lambda j: (tf(j), 0, 0, 0))],
        out_shape=[jax.ShapeDtypeStruct((n, D_MODEL), _F32),
                   jax.ShapeDtypeStruct((n, PN_WIDTH), _BF16),
                   jax.ShapeDtypeStruct((b, PT_ROWS, seq), _BF16),
                   jax.ShapeDtypeStruct((n, MLSTM_WIDTH), _BF16),
                   jax.ShapeDtypeStruct((b, MLSTM_WIDTH, seq), _BF16),
                   jax.ShapeDtypeStruct((b, 2 * ST_ROWS, seq), _F32),
                   jax.ShapeDtypeStruct((n, LANES), _BF16),
                   jax.ShapeDtypeStruct((n // CHUNK, N_MLSTM_HEADS, CHUNK, CHUNK), _BF16)],
        scratch_shapes=[pltpu.VMEM((tm + 2 * HALO, 2 * MLSTM_WIDTH), _F32),
                        pltpu.VMEM((N_GATES, tm), _F32)],
        compiler_params=pltpu.CompilerParams(dimension_semantics=("arbitrary",),
                                             vmem_limit_bytes=VMEM_LIMIT),
        name="ffn1_proj_prep",
    )(x2, x2, x2, g1, wgu, wd, gmix, wqk, wn, wt, bgate, wconv)


def _attn_scores(qt_ref, kcat, blk, h):
    zeros = jnp.zeros((HEAD_DIM, BLOCK), _BF16)
    keys = kcat[blk * BLOCK:(blk + 3) * BLOCK, :]
    qt = (qt_ref[h * HEAD_DIM:(h + 1) * HEAD_DIM, blk * BLOCK:(blk + 1) * BLOCK]
          * (HEAD_DIM ** -0.5))
    qtz = jnp.concatenate([qt, zeros] if h < GQA_GROUP else [zeros, qt], axis=0)
    return jnp.dot(keys, qtz, preferred_element_type=_F32)


def _attn_softmax(s, sink_ref, bias_ref, h, edge_ok):
    s = s + bias_ref[h]
    if edge_ok is not None:
        s = jnp.where(edge_ok, s, NEG)
    sk = sink_ref[h]
    m = jnp.maximum(jnp.max(s, axis=0, keepdims=True), sk)
    p = jnp.exp(s - m)
    den = jnp.sum(p, axis=0, keepdims=True) + jnp.exp(sk - m)
    return p.astype(_BF16), 1.0 / den


def _attn_values(p, rden, vcat, blk, h):
    g = h // GQA_GROUP
    vtg = vcat[g * HEAD_DIM:(g + 1) * HEAD_DIM, blk * BLOCK:(blk + 3) * BLOCK]
    return jnp.dot(vtg, p, preferred_element_type=_F32) * rden


def _mix_ffn_kernel(sink_ref, qt_ref, kp_ref, kc_ref, kn_ref, vp_ref, vc_ref, vn_ref, bias_ref,
                    h_ref, hf_ref, hb_ref, o_ref, gout_ref, wout_ref, g2_ref, wgu_ref, wd_ref,
                    gf_ref, y_ref, attn_ref, *, tiles_per_seq):
    j = pl.program_id(0)
    tm = h_ref.shape[0]
    nblk = tm // BLOCK

    @pl.when(j == 0)
    def _():
        attn_ref[...] = jnp.zeros_like(attn_ref)

    pos = jnp.minimum(j, pl.num_programs(0) - 2) % tiles_per_seq
    kcat = jnp.concatenate([kp_ref[...], kc_ref[...], kn_ref[...]], axis=0)
    vcat = jnp.concatenate([vp_ref[...], vc_ref[...], vn_ref[...]], axis=1)
    krow = lax.broadcasted_iota(jnp.int32, (3 * BLOCK, BLOCK), 0)
    edge = {0: (krow >= BLOCK) | (pos > 0)}
    ok_next = (krow < 2 * BLOCK) | (pos < tiles_per_seq - 1)
    edge[nblk - 1] = ok_next if nblk > 1 else edge[0] & ok_next
    units = [(blk, h) for blk in range(nblk) for h in range(N_ATTN_HEADS)]
    n_slices = len(FF_SLICES)
    per = -(-len(units) // n_slices)
    groups = [units[i * per:(i + 1) * per] for i in range(n_slices)]
    attn_out = {}
    scored = [_attn_scores(qt_ref, kcat, blk, hh) for blk, hh in groups[0]]
    weighted = []

    lane = lax.broadcasted_iota(jnp.int32, (tm, LANES), 1)
    lo_half = lane < HEAD_DIM
    hm_tiles = []
    for p in range(N_MLSTM_HEADS // 2):
        sl = slice(p * LANES, (p + 1) * LANES)
        hm = (jax.nn.sigmoid(o_ref[:, sl].astype(_F32))
              * (hf_ref[:, sl].astype(_F32) + hb_ref[:, sl].astype(_F32)))
        sq = hm * hm
        ss_lo = jnp.sum(jnp.where(lo_half, sq, 0.0), axis=1, keepdims=True)
        ss_hi = jnp.sum(jnp.where(lo_half, 0.0, sq), axis=1, keepdims=True)
        ms = jnp.where(lo_half, ss_lo, ss_hi) * (1.0 / HEAD_DIM)
        hm_tiles.append((hm * lax.rsqrt(ms + EPS) * gout_ref[:, sl]).astype(_BF16))
    hm_all = jnp.concatenate(hm_tiles, axis=1)
    h = (h_ref[...]
         + jnp.dot(attn_ref[...], wout_ref[0:ATTN_WIDTH, :], preferred_element_type=_F32)
         + jnp.dot(hm_all, wout_ref[ATTN_WIDTH:MIX_WIDTH, :], preferred_element_type=_F32))
    hn = _rmsnorm(h, g2_ref[...]).astype(_BF16)
    acc = None
    for t, (lo, hi) in enumerate(FF_SLICES):
        upcoming =([_attn_scores(qt_ref, kcat, blk, hh) for blk, hh in groups[t + 1]]
                    if t + 1 < n_slices else [])
        g = jnp.dot(hn, wgu_ref[:, lo:hi], preferred_element_type=_F32)
        u = jnp.dot(hn, wgu_ref[:, D_FF + lo:D_FF + hi], preferred_element_type=_F32)
        for (blk, hh), p, rden in weighted:
            attn_out[blk, hh] = _attn_values(p, rden, vcat, blk, hh)
        weighted = [((blk, hh),) + _attn_softmax(s, sink_ref, bias_ref, hh, edge.get(blk))
                    for (blk, hh), s in zip(groups[t], scored)]
        act = (jax.nn.silu(g) * u).astype(_BF16)
        part = jnp.dot(act, wd_ref[lo:hi, :], preferred_element_type=_F32)
        acc = part if acc is None else acc + part
        scored = upcoming
    for (blk, hh), p, rden in weighted:
        attn_out[blk, hh] = _attn_values(p, rden, vcat, blk, hh)
    h = h + 0.5 * acc
    y_ref[...] = _rmsnorm(h, gf_ref[...])

    for blk in range(nblk):
        heads = [attn_out[blk, hh] for hh in range(N_ATTN_HEADS)]
        attn_ref[blk * BLOCK:(blk + 1) * BLOCK, :] = (
            jnp.concatenate(heads, axis=0).T.astype(_BF16))


def _mix_ffn(seq, pn2, pt3, h2, hf2, hb2, sink, bias_t, gout, wout, g2, wgu, wd, gf):
    n = h2.shape[0]
    tm = TOKEN_TILE
    nt = n // tm
    per_seq = seq // tm
    per_blk = tm // BLOCK
    nblocks = n // BLOCK
    blk_per_seq = seq // BLOCK
    ta = lambda j: jnp.minimum(j, nt - 1)
    tf = lambda j: jnp.maximum(j - 1, 0)
    kcol = PN_K_A // KV_WIDTH
    vrow = PT_V_A // KV_WIDTH
    kprev = lambda j: jnp.maximum(ta(j) * per_blk - 1, 0)
    knext = lambda j: jnp.minimum((ta(j) + 1) * per_blk, nblocks - 1)
    vmain = lambda j: (ta(j) // per_seq, vrow, ta(j) % per_seq)
    vprev = lambda j: (kprev(j) // blk_per_seq, vrow, kprev(j) % blk_per_seq)
    vnext = lambda j: (knext(j) // blk_per_seq, vrow, knext(j) % blk_per_seq)
    row = lambda w, c=0: pl.BlockSpec((tm, w), lambda j: (tf(j), c))
    return pl.pallas_call(
        functools.partial(_mix_ffn_kernel, tiles_per_seq=per_seq),
        grid=(nt + 1,),
        in_specs=[pl.BlockSpec(memory_space=pltpu.SMEM),
                  pl.BlockSpec((None, ATTN_WIDTH, tm),
                               lambda j: (ta(j) // per_seq, PT_Q_A // ATTN_WIDTH, ta(j) % per_seq)),
                  pl.BlockSpec((BLOCK, KV_WIDTH), lambda j: (kprev(j), kcol)),
                  pl.BlockSpec((tm, KV_WIDTH), lambda j: (ta(j), kcol)),
                  pl.BlockSpec((BLOCK, KV_WIDTH), lambda j: (knext(j), kcol)),
                  pl.BlockSpec((None, KV_WIDTH, BLOCK), vprev),
                  pl.BlockSpec((None, KV_WIDTH, tm), vmain),
                  pl.BlockSpec((None, KV_WIDTH, BLOCK), vnext),
                  _const_spec((N_ATTN_HEADS, 3 * BLOCK, BLOCK)),
                  row(D_MODEL), row(MLSTM_WIDTH), row(MLSTM_WIDTH),
                  row(MLSTM_WIDTH, PN_O_M // MLSTM_WIDTH),
                  _const_spec((1, MLSTM_WIDTH)), _const_spec((MIX_WIDTH, D_MODEL)),
                  _const_spec((1, D_MODEL)), _const_spec((D_MODEL, 2 * D_FF)),
                  _const_spec((D_FF, D_MODEL)), _const_spec((1, D_MODEL))],
        out_specs=row(D_MODEL),
        out_shape=jax.ShapeDtypeStruct((n, D_MODEL), _F32),
        scratch_shapes=[pltpu.VMEM((tm, ATTN_WIDTH), _BF16)],
        compiler_params=pltpu.CompilerParams(dimension_semantics=("arbitrary",),
                                             vmem_limit_bytes=VMEM_LIMIT),
        name="attn_out_ffn2",
    )(sink, pt3, pn2, pn2, pn2, pt3, pt3, pt3, bias_t, h2, hf2, hb2, pn2,
      gout, wout, g2, wgu, wd, gf)


def _chunk_rows(d, st, m_prev):
    nh = N_MLSTM_HEADS
    a = st[ST_A:ST_A + nh]
    gl = st[ST_GL:ST_GL + nh]
    a_end = st[ST_AEND:ST_AEND + nh]
    m_loc = st[ST_MLOC:ST_MLOC + nh]
    g = jnp.maximum(m_prev, gl)
    m_new = jnp.maximum(a_end + m_prev, m_loc)
    row8 = lax.broadcasted_iota(jnp.int32, (nh, CHUNK), 0)
    zero8 = jnp.zeros((nh, CHUNK), _F32)
    zero_half = jnp.zeros((DEC_K, CHUNK), _F32)
    decay_rhs = []
    for h in range(nh):
        hot = row8 == h
        one = jnp.where(hot, 1.0, 0.0)
        blk = jnp.concatenate([one, one, one] + [jnp.where(hot, t, 0.0) for t in _split3(-g * LOG2E)]
                              + [zero8, zero8], axis=0)
        x = jnp.concatenate([zero_half, blk] if d else [blk, zero_half], axis=0)
        decay_rhs.append(x.astype(_BF16))
    return dict(e=st[ST_E:ST_E + nh], decay_rhs=decay_rhs,
                inter=jnp.exp(m_prev - g),
                floor=jnp.exp(-(a + g)),
                m_new=m_new,
                s_old=jnp.exp(a_end + m_prev - m_new), s_loc=jnp.exp(m_loc - m_new))


def _chunk_input_matmuls(d, rows, k, qt, vt, dec):
    nh = N_MLSTM_HEADS
    zero_head = jnp.zeros((HEAD_DIM, CHUNK), _BF16)
    e = rows["e"]
    heads = []
    for h in range(nh):
        p, par = divmod(h, 2)
        hs = slice(h * HEAD_DIM, (h + 1) * HEAD_DIM)
        qt_h = qt[hs, :]
        qtz = jnp.concatenate([zero_head, qt_h] if par else [qt_h, zero_head], axis=0)
        k_pair = k[:, p * LANES:(p + 1) * LANES]
        vt_h = vt[hs, :]
        heads.append(dict(
            qtz=qtz, vt=vt_h,
            dexp=jnp.dot(dec, rows["decay_rhs"][h], preferred_element_type=_F32),
            c_loc=jnp.dot((vt_h.astype(_F32) * e[h:h + 1, :]).astype(_BF16), k_pair,
                          preferred_element_type=_F32)))
    n_loc = jnp.dot(e.astype(_BF16), k, preferred_element_type=_F32)
    return heads, n_loc


def _chunk_state_matmuls(heads, qt, c_state, n_state):
    for hd, cz in zip(heads, c_state):
        hd["nd_state"] = jnp.dot(cz.astype(_BF16), hd["qtz"], preferred_element_type=_F32)
    return jnp.dot(n_state.astype(_BF16), qt, preferred_element_type=_F32)


def _state_update(rows, heads, n_loc, c_state, n_state):
    nh = N_MLSTM_HEADS
    s_old, s_loc = rows["s_old"], rows["s_loc"]
    c_new = [s_old[h:h + 1, :] * cz + s_loc[h:h + 1, :] * hd["c_loc"]
             for h, (hd, cz) in enumerate(zip(heads, c_state))]
    rep = MLSTM_WIDTH // LANES
    head_of_lane = lax.broadcasted_iota(jnp.int32, (nh, MLSTM_WIDTH), 1) // HEAD_DIM
    own = head_of_lane == lax.broadcasted_iota(jnp.int32, (nh, MLSTM_WIDTH), 0)
    n_new = (jnp.concatenate([s_old] * rep, axis=1) * n_state
             + jnp.concatenate([s_loc] * rep, axis=1) * jnp.where(own, n_loc, 0.0))
    return c_new, n_new


def _chunk_weighted_values(d, heads, s_ref, i):
    row = lax.broadcasted_iota(jnp.int32, (CHUNK, CHUNK), 0)
    lane = lax.broadcasted_iota(jnp.int32, (CHUNK, CHUNK), 1)
    keep = (row >= lane) if d else (row <= lane)
    for h, hd in enumerate(heads):
        qk = s_ref[i, h].astype(_F32) * jnp.exp2(jnp.where(keep, hd["dexp"], NEG))
        hd["den_intra"] = jnp.sum(qk, axis=0, keepdims=True)
        hd["nd_intra"] = jnp.dot(hd["vt"], qk.astype(_BF16), preferred_element_type=_F32)


def _chunk_finish(rows, heads, den_state, out_ref, rs):
    inter, floor = rows["inter"], rows["floor"]
    for p in range(N_MLSTM_HEADS // 2):
        tiles = []
        for h in (2 * p, 2 * p + 1):
            hd = heads[h]
            it = inter[h:h + 1, :]
            den = hd["den_intra"] + it * den_state[h:h + 1, :]
            num = hd["nd_intra"] + it * hd["nd_state"]
            tiles.append(num / jnp.maximum(jnp.abs(den), floor[h:h + 1, :]))
        out_ref[rs, p * LANES:(p + 1) * LANES] = jnp.concatenate(tiles, axis=0).T.astype(_BF16)


def _mlstm_seq_kernel(kf_ref, qtf_ref, vtf_ref, decf_ref, stf_ref, sf_ref,
                      kb_ref, qtb_ref, vtb_ref, decb_ref, stb_ref, sb_ref,
                      hf_ref, hb_ref, c_ref, n_ref, m_ref):
    @pl.when(pl.program_id(1) == 0)
    def _():
        c_ref[...] = jnp.zeros_like(c_ref)
        n_ref[...] = jnp.zeros_like(n_ref)
        m_ref[...] = jnp.full_like(m_ref, NEG)

    nh = N_MLSTM_HEADS
    n_sub = kf_ref.shape[0] // CHUNK
    refs = [(kf_ref, qtf_ref, vtf_ref, decf_ref, stf_ref, hf_ref),
            (kb_ref, qtb_ref, vtb_ref, decb_ref, stb_ref, hb_ref)]
    score_refs = [sf_ref, sb_ref]
    order = [list(range(n_sub)), list(range(n_sub - 1, -1, -1))]
    c_state = [[c_ref[d, h] for h in range(nh)] for d in range(2)]
    n_state = [n_ref[d] for d in range(2)]
    rows = [[], []]
    for d in range(2):
        m = m_ref[d]
        for i in order[d]:
            rows[d].append(_chunk_rows(d, refs[d][4][:, i * CHUNK:(i + 1) * CHUNK], m))
            m = rows[d][-1]["m_new"]
        m_ref[d] = m

    work = [[None] * n_sub, [None] * n_sub]

    def start(step):
        for d in range(2):
            k_ref, qt_ref, vt_ref, dec_ref, _, _ = refs[d]
            ts = slice(order[d][step] * CHUNK, (order[d][step] + 1) * CHUNK)
            heads, n_loc = _chunk_input_matmuls(d, rows[d][step], k_ref[ts, :], qt_ref[:, ts],
                                                vt_ref[:, ts], dec_ref[ts, :])
            work[d][step] = dict(heads=heads, n_loc=n_loc, ts=ts)
        for d in range(2):
            w = work[d][step]
            w["den_state"] = _chunk_state_matmuls(w["heads"], refs[d][1][:, w["ts"]],
                                                  c_state[d], n_state[d])
        for d in range(2):
            w = work[d][step]
            c_state[d], n_state[d] = _state_update(rows[d][step], w["heads"], w["n_loc"],
                                                   c_state[d], n_state[d])

    def weigh(step):
        for d in range(2):
            _chunk_weighted_values(d, work[d][step]["heads"], score_refs[d], order[d][step])

    def finish(step):
        for d in range(2):
            w = work[d][step]
            _chunk_finish(rows[d][step], w["heads"], w["den_state"], refs[d][5], w["ts"])

    for step in range(n_sub + 2):
        if step < n_sub:
            start(step)
        if 1 <= step <= n_sub:
            weigh(step - 1)
        if step >= 2:
            finish(step - 2)
    for d in range(2):
        for h in range(nh):
            c_ref[d, h] = c_state[d][h]
        n_ref[d] = n_state[d]


def _mlstm_seq(k3, qt3, pt3, dec3, st3, sc5):
    b, s, _ = k3.shape
    sub = SEQ_CHUNKS if (s // CHUNK) % SEQ_CHUNKS == 0 else 1
    blk = sub * CHUNK
    nc = s // blk
    vrow = PT_V_M // MLSTM_WIDTH

    def specs(pos, d):
        return [pl.BlockSpec((None, blk, MLSTM_WIDTH), lambda bi, c: (bi, pos(c), 0)),
                pl.BlockSpec((None, MLSTM_WIDTH, blk), lambda bi, c: (bi, 0, pos(c))),
                pl.BlockSpec((None, MLSTM_WIDTH, blk), lambda bi, c: (bi, vrow, pos(c))),
                pl.BlockSpec((None, blk, LANES), lambda bi, c: (bi, pos(c), 0)),
                pl.BlockSpec((None, ST_ROWS, blk), lambda bi, c: (bi, d, pos(c))),
                pl.BlockSpec((None, sub, N_MLSTM_HEADS, CHUNK, CHUNK),
                             lambda bi, c: (bi, pos(c), 0, 0, 0))]

    fwd = lambda c: c
    bwd = lambda c: nc - 1 - c
    out = lambda pos: pl.BlockSpec((None, blk, MLSTM_WIDTH), lambda bi, c: (bi, pos(c), 0))
    args = (k3, qt3, pt3, dec3, st3, sc5)
    return pl.pallas_call(
        _mlstm_seq_kernel,
        grid=(b, nc),
        in_specs=specs(fwd, 0) + specs(bwd, 1),
        out_specs=[out(fwd), out(bwd)],
        out_shape=[jax.ShapeDtypeStruct((b, s, MLSTM_WIDTH), _BF16)] * 2,
        scratch_shapes=[pltpu.VMEM((2, N_MLSTM_HEADS, HEAD_DIM, LANES), _F32),
                        pltpu.VMEM((2, N_MLSTM_HEADS, MLSTM_WIDTH), _F32),
                        pltpu.VMEM((2, N_MLSTM_HEADS, LANES), _F32)],
        compiler_params=pltpu.CompilerParams(dimension_semantics=("parallel", "arbitrary")),
        name="mlstm_seq",
    )(*args, *args)


def _t5_bucket(rel):
    nb = N_BUCKETS // 2
    ret = (rel > 0).astype(np.int32) * nb
    n = np.abs(rel)
    max_exact = nb // 2
    large = max_exact + (np.log(np.maximum(n, 1) / max_exact)
                         / math.log(MAX_DISTANCE / max_exact) * (nb - max_exact)).astype(np.int32)
    large = np.minimum(large, nb - 1)
    return (ret + np.where(n < max_exact, n, large)).astype(np.int32)


def _prepare_params(g_ffn1, w_ffn1_gu, w_ffn1_down, g_mix, w_in, w_conv, b_gates, attn_sink,
                    g_mlstm_out, w_out, g_ffn2, w_ffn2_gu, w_ffn2_down, rel_table, g_final):
    row = lambda g: g.reshape(1, -1).astype(_F32)
    offs = np.cumsum((0,) + SPLIT_SIZES)
    q_a, k_a, v_a, q_m, k_m, v_m, o_m, gate = (w_in[0][:, offs[i]:offs[i + 1]] for i in range(8))
    wqk = jnp.concatenate([q_m, k_m], axis=1).astype(_BF16)
    wn = jnp.concatenate([o_m, k_a], axis=1).astype(_BF16)
    gate_pad = jnp.pad(gate, ((0, 0), (0, GATE_ROWS_PAD - N_GATES)))
    wt = jnp.concatenate([q_a, v_m, v_a, gate_pad], axis=1).T.astype(_BF16)
    bgate = b_gates[0].reshape(N_GATES, 1).astype(_F32)
    wconv = jnp.pad(w_conv[0].astype(_F32), ((0, SUBLANES - CONV_WIDTH), (0, 0)))
    kj = np.arange(3 * BLOCK)[:, None]
    qi = np.arange(BLOCK)[None, :]
    bucket = jnp.asarray(_t5_bucket((kj - BLOCK) - qi).reshape(-1))
    onehot = (bucket[None, :] == jnp.arange(N_BUCKETS)[:, None]).astype(_F32)
    bias_t = jnp.dot(rel_table.astype(_F32).T, onehot, precision=lax.Precision.HIGHEST)
    bias_t = bias_t.reshape(N_ATTN_HEADS, 3 * BLOCK, BLOCK)
    window = np.abs((kj - BLOCK) - qi) <= WINDOW
    bias_t = jnp.where(jnp.asarray(window)[None], bias_t, NEG)
    return dict(
        g1=row(g_ffn1[0]), wgu1=w_ffn1_gu[0].astype(_BF16), wd1=w_ffn1_down[0].astype(_BF16),
        gmix=row(g_mix[0]), wqk=wqk, wn=wn, wt=wt, bgate=bgate, wconv=wconv,
        sink=attn_sink[0].astype(_F32), bias_t=bias_t, gout=row(g_mlstm_out[0]),
        wout=w_out[0].astype(_BF16), g2=row(g_ffn2[0]), wgu2=w_ffn2_gu[0].astype(_BF16),
        wd2=w_ffn2_down[0].astype(_BF16), gf=row(g_final))


def _trunk(x, p):
    b, s, d = x.shape
    n = b * s
    h2, pn2, pt3, k2, qt3, st3, dec2, sc4 = _ffn_proj(
        x.reshape(n, d), s, p["g1"], p["wgu1"], p["wd1"], p["gmix"], p["wqk"], p["wn"], p["wt"],
        p["bgate"], p["wconv"])
    hf3, hb3 = _mlstm_seq(k2.reshape(b, s, MLSTM_WIDTH), qt3, pt3, dec2.reshape(b, s, LANES), st3,
                          sc4.reshape(b, s // CHUNK, N_MLSTM_HEADS, CHUNK, CHUNK))
    y2 = _mix_ffn(s, pn2, pt3, h2, hf3.reshape(n, MLSTM_WIDTH), hb3.reshape(n, MLSTM_WIDTH),
                  p["sink"], p["bias_t"], p["gout"], p["wout"], p["g2"], p["wgu2"], p["wd2"],
                  p["gf"])
    return y2.reshape(b, s, d)


def kernel(x_prompt, x_sample, g_ffn1, w_ffn1_gu, w_ffn1_down, g_mix, w_in, w_conv, b_gates,
           attn_sink, g_mlstm_out, w_out, g_ffn2, w_ffn2_gu, w_ffn2_down, rel_bias_table, g_final):
    p = _prepare_params(g_ffn1, w_ffn1_gu, w_ffn1_down, g_mix, w_in, w_conv, b_gates, attn_sink,
                        g_mlstm_out, w_out, g_ffn2, w_ffn2_gu, w_ffn2_down, rel_bias_table, g_final)
    return (_trunk(x_prompt, p), _trunk(x_sample, p))
```

```python
import functools
import math

import jax
import jax.numpy as jnp
import numpy as np
from jax import lax
from jax.experimental import pallas as pl
from jax.experimental.pallas import tpu as pltpu

D_MODEL = 1024
HEAD_DIM = 64
N_ATTN_HEADS = 8
N_KV_HEADS = 2
GQA_GROUP = N_ATTN_HEADS // N_KV_HEADS
ATTN_WIDTH = N_ATTN_HEADS * HEAD_DIM
KV_WIDTH = N_KV_HEADS * HEAD_DIM
N_MLSTM_HEADS = 8
MLSTM_WIDTH = N_MLSTM_HEADS * HEAD_DIM
MIX_WIDTH = ATTN_WIDTH + MLSTM_WIDTH
WINDOW = 128
BLOCK = 128
N_BUCKETS = 32
MAX_DISTANCE = 128
CHUNK = 128
CONV_WIDTH = 5
D_FF = 2816
EPS = 1e-6
NEG = -1e30
LOG2E = math.log2(math.e)
SPLIT_SIZES = (ATTN_WIDTH, KV_WIDTH, KV_WIDTH, MLSTM_WIDTH, MLSTM_WIDTH, MLSTM_WIDTH,
               MLSTM_WIDTH, 4 * N_MLSTM_HEADS)
N_GATES = 4 * N_MLSTM_HEADS

LANES = 128
SUBLANES = 8
BF16_SUBLANES = 16
MXU_DIM = 256

PN_O_M = 0
PN_K_A = PN_O_M + MLSTM_WIDTH
PN_WIDTH = PN_K_A + KV_WIDTH
PT_Q_A = 0
PT_V_M = ATTN_WIDTH
PT_V_A = PT_V_M + MLSTM_WIDTH
PT_ROWS = PT_V_A + KV_WIDTH
GATE_ROWS_PAD = N_GATES

HALO = BF16_SUBLANES
TOKEN_TILE = 512
SEQ_CHUNKS = 8
FF_SLICE = MXU_DIM
FF_SLICES = tuple((lo, min(lo + FF_SLICE, D_FF)) for lo in range(0, D_FF, FF_SLICE))
VMEM_BYTES = 64 * 1024 * 1024
VMEM_LIMIT = VMEM_BYTES * 7 // 8

ST_A, ST_GL, ST_E, ST_AEND, ST_MLOC = (i * N_MLSTM_HEADS for i in range(5))
ST_ROWS = 5 * N_MLSTM_HEADS
DEC_K = LANES // 2

_F32 = jnp.float32
_BF16 = jnp.bfloat16


def _rmsnorm(x, g):
    y = x * lax.rsqrt(jnp.mean(x * x, axis=-1, keepdims=True) + EPS)
    return y * g


def _split3(x):
    hi = x.astype(_BF16).astype(_F32)
    rem = x - hi
    mid = rem.astype(_BF16).astype(_F32)
    lo = (rem - mid).astype(_BF16).astype(_F32)
    return hi, mid, lo


def _const_spec(shape):
    nd = len(shape)
    return pl.BlockSpec(shape, lambda *_: (0,) * nd, pipeline_mode=pl.Buffered(1))


def _conv_unit(qk_ref, wconv_ref, c, r0):
    cs = slice(c * LANES, (c + 1) * LANES)
    w = wconv_ref[:, cs]
    y = None
    for tap in range(CONV_WIDTH):
        lo = r0 + HALO - CONV_WIDTH // 2 + tap
        term = qk_ref[lo:lo + CHUNK, cs] * w[tap:tap + 1, :]
        y = term if y is None else y + term
    return jax.nn.silu(y)


def _gate_scan_start(gt, tri):
    nh = N_MLSTM_HEADS
    out = []
    for d in range(2):
        ig = gt[2 * d * nh:(2 * d + 1) * nh, :]
        fg = gt[(2 * d + 1) * nh:(2 * d + 2) * nh, :]
        lf = jnp.minimum(fg, 0.0) - jnp.log1p(jnp.exp(-jnp.abs(fg)))
        parts = jnp.concatenate(_split3(lf), axis=0).astype(_BF16)
        sums = jnp.dot(parts, tri[d], preferred_element_type=_F32)
        out.append((ig, sums[0:nh] + sums[nh:2 * nh] + sums[2 * nh:3 * nh]))
    return out


def _gate_scan_finish(scans, st_ref, dec_ref, cs):
    nh = N_MLSTM_HEADS
    lane8 = lax.broadcasted_iota(jnp.int32, (nh, CHUNK), 1)
    ones = jnp.ones((3 * nh, CHUNK), _F32)
    zpad = jnp.zeros((DEC_K - 6 * nh, CHUNK), _F32)
    dec_rows = []
    for d, (ig, a) in enumerate(scans):
        r = ig - a
        gl = r
        sh = 1
        while sh < CHUNK:
            if d == 0:
                gl = jnp.maximum(gl, jnp.where(lane8 >= sh, pltpu.roll(gl, sh, 1), NEG))
            else:
                gl = jnp.maximum(
                    gl, jnp.where(lane8 < CHUNK - sh, pltpu.roll(gl, CHUNK - sh, 1), NEG))
            sh *= 2
        far = CHUNK - 1 if d == 0 else 0
        a_end = jnp.broadcast_to(a[:, far:far + 1], (nh, CHUNK))
        m_loc = a_end + jnp.broadcast_to(gl[:, far:far + 1], (nh, CHUNK))
        e = jnp.exp(a_end + r - m_loc)
        st_ref[d * ST_ROWS:(d + 1) * ST_ROWS, cs] = jnp.concatenate(
            [a, gl, e, a_end, m_loc], axis=0)
        dec_rows += list(_split3(r * LOG2E)) + [ones, zpad]
    dec_ref[cs, :] = jnp.concatenate(dec_rows, axis=0).T.astype(_BF16)


def _ffn_proj_kernel(xm_ref, xn_ref, g1_ref, wgu_ref, wd_ref, gmix_ref, wqk_ref, wn_ref,
                     wt_ref, bgate_ref, wconv_ref,
                     h_ref, pn_ref, pt_ref, k_ref, qt_ref, st_ref, dec_ref, s_ref,
                     qk_scr, gt_scr, *, tiles_per_seq):
    j = pl.program_id(0)
    tm = xm_ref.shape[0]
    n_chunks = tm // CHUNK

    @pl.when(j == 0)
    def _():
        qk_scr[...] = jnp.zeros_like(qk_scr)
        gt_scr[...] = jnp.zeros_like(gt_scr)

    srow = lax.broadcasted_iota(jnp.int32, (CHUNK, CHUNK), 0)
    slane = lax.broadcasted_iota(jnp.int32, (CHUNK, CHUNK), 1)
    tri = [(srow <= slane).astype(_BF16), (srow >= slane).astype(_BF16)]
    scans = [_gate_scan_start(gt_scr[:, c * CHUNK:(c + 1) * CHUNK], tri)
             for c in range(n_chunks)]

    n_pairs = N_MLSTM_HEADS // 2
    zero_head = jnp.zeros((HEAD_DIM, CHUNK), _BF16)

    def conv_unit(p, c):
        r0 = c * CHUNK
        qt = _conv_unit(qk_scr, wconv_ref, p, r0).T.astype(_BF16)
        k = (_conv_unit(qk_scr, wconv_ref, n_pairs + p, r0) * (HEAD_DIM ** -0.5)).astype(_BF16)
        qt_ref[p * LANES:(p + 1) * LANES, r0:r0 + CHUNK] = qt
        k_ref[r0:r0 + CHUNK, p * LANES:(p + 1) * LANES] = k
        return p, c, k, qt

    def score_unit(p, c, k, qt):
        for par in range(2):
            half = qt[par * HEAD_DIM:(par + 1) * HEAD_DIM]
            qtz = jnp.concatenate([zero_head, half] if par else [half, zero_head], axis=0)
            s_ref[c, 2 * p + par] = jnp.dot(k, qtz, preferred_element_type=_F32).astype(_BF16)

    conv_units = [functools.partial(conv_unit, p, c)
                  for c in range(n_chunks) for p in range(n_pairs)]
    scan_units = [functools.partial(_gate_scan_finish, scans[c], st_ref, dec_ref,
                                    slice(c * CHUNK, (c + 1) * CHUNK)) for c in range(n_chunks)]

    pos = jnp.minimum(j, pl.num_programs(0) - 2) % tiles_per_seq
    x = jnp.concatenate([xm_ref[...], xn_ref[...]], axis=0)
    xn = _rmsnorm(x, g1_ref[...]).astype(_BF16)
    per_slice = -(-len(conv_units) // (len(FF_SLICES) - 1))
    convolved = []
    acc = None
    for lo, hi in FF_SLICES:
        for done in convolved:
            score_unit(*done)
        g = jnp.dot(xn, wgu_ref[:, lo:hi], preferred_element_type=_F32)
        u = jnp.dot(xn, wgu_ref[:, D_FF + lo:D_FF + hi], preferred_element_type=_F32)
        convolved = [unit() for unit in conv_units[:per_slice]]
        del conv_units[:per_slice]
        if not convolved and scan_units:
            scan_units.pop(0)()
        act = (jax.nn.silu(g) * u).astype(_BF16)
        part = jnp.dot(act, wd_ref[lo:hi, :], preferred_element_type=_F32)
        acc = part if acc is None else acc + part
    for done in convolved:
        score_unit(*done)
    while scan_units:
        scan_units.pop(0)()
    h = x + 0.5 * acc
    h_ref[...] = h[0:tm]
    un = _rmsnorm(h, gmix_ref[...]).astype(_BF16)
    qk = jnp.dot(un, wqk_ref[...], preferred_element_type=_F32)
    prev_tail = qk_scr[tm:tm + HALO, :]
    qk_scr[0:HALO, :] = jnp.where(pos > 0, prev_tail, 0.0)
    qk_scr[HALO:HALO + tm, :] = qk[0:tm]
    qk_scr[HALO + tm:, :] = jnp.where(pos < tiles_per_seq - 1, qk[tm:], 0.0)
    un_c = un[0:tm]
    pn_ref[...] = jnp.dot(un_c, wn_ref[...], preferred_element_type=_F32).astype(_BF16)
    t = lax.dot_general(wt_ref[...], un_c, (((1,), (1,)), ((), ())),
                        preferred_element_type=_F32)
    pt_ref[...] = t[:PT_ROWS].astype(_BF16)
    gt_scr[...] = t[PT_ROWS:PT_ROWS + N_GATES] + bgate_ref[...]


def _ffn_proj(x2, seq, g1, wgu, wd, gmix, wqk, wn, wt, bgate, wconv):
    n = x2.shape[0]
    tm = TOKEN_TILE
    nt = n // tm
    per_seq = seq // tm
    per = tm // HALO
    nhalo = n // HALO
    ta = lambda j: jnp.minimum(j, nt - 1)
    tf = lambda j: jnp.maximum(j - 1, 0)
    rowa = lambda w: pl.BlockSpec((tm, w), lambda j: (ta(j), 0))
    rowf = lambda w: pl.BlockSpec((tm, w), lambda j: (tf(j), 0))
    cola = lambda r: pl.BlockSpec((None, r, tm), lambda j: (ta(j) // per_seq, 0, ta(j) % per_seq))
    colf = lambda r: pl.BlockSpec((None, r, tm), lambda j: (tf(j) // per_seq, 0, tf(j) % per_seq))
    b = n // seq
    return pl.pallas_call(
        functools.partial(_ffn_proj_kernel, tiles_per_seq=per_seq),
        grid=(nt + 1,),
        in_specs=[rowa(D_MODEL),
                  pl.BlockSpec((HALO, D_MODEL),
                               lambda j: (jnp.minimum((ta(j) + 1) * per, nhalo - 1), 0)),
                  _const_spec((1, D_MODEL)), _const_spec((D_MODEL, 2 * D_FF)),
                  _const_spec((D_FF, D_MODEL)), _const_spec((1, D_MODEL)),
                  _const_spec((D_MODEL, 2 * MLSTM_WIDTH)),
                  _const_spec((D_MODEL, PN_WIDTH)),
                  _const_spec((PT_ROWS + GATE_ROWS_PAD, D_MODEL)),
                  _const_spec((N_GATES, 1)), _const_spec((SUBLANES, 2 * MLSTM_WIDTH))],
        out_specs=[rowa(D_MODEL), rowa(PN_WIDTH), cola(PT_ROWS),
                   rowf(MLSTM_WIDTH), colf(MLSTM_WIDTH), colf(2 * ST_ROWS), rowf(LANES),
                   pl.BlockSpec((tm // CHUNK, N_MLSTM_HEADS, CHUNK, CHUNK),
                                lambda j: (tf(j), 0, 0, 0))],
        out_shape=[jax.ShapeDtypeStruct((n, D_MODEL), _F32),
                   jax.ShapeDtypeStruct((n, PN_WIDTH), _BF16),
                   jax.ShapeDtypeStruct((b, PT_ROWS, seq), _BF16),
                   jax.ShapeDtypeStruct((n, MLSTM_WIDTH), _BF16),
                   jax.ShapeDtypeStruct((b, MLSTM_WIDTH, seq), _BF16),
                   jax.ShapeDtypeStruct((b, 2 * ST_ROWS, seq), _F32),
                   jax.ShapeDtypeStruct((n, LANES), _BF16),
                   jax.ShapeDtypeStruct((n // CHUNK, N_MLSTM_HEADS, CHUNK, CHUNK), _BF16)],
        scratch_shapes=[pltpu.VMEM((tm + 2 * HALO, 2 * MLSTM_WIDTH), _F32),
                        pltpu.VMEM((N_GATES, tm), _F32)],
        compiler_params=pltpu.CompilerParams(dimension_semantics=("arbitrary",),
                                             vmem_limit_bytes=VMEM_LIMIT),
        name="ffn1_proj_prep",
    )(x2, x2, g1, wgu, wd, gmix, wqk, wn, wt, bgate, wconv)


def _attn_scores(qt_ref, kcat, blk, h):
    zeros = jnp.zeros((HEAD_DIM, BLOCK), _BF16)
    keys = kcat[blk * BLOCK:(blk + 3) * BLOCK, :]
    qt = (qt_ref[h * HEAD_DIM:(h + 1) * HEAD_DIM, blk * BLOCK:(blk + 1) * BLOCK]
          * (HEAD_DIM ** -0.5))
    qtz = jnp.concatenate([qt, zeros] if h < GQA_GROUP else [zeros, qt], axis=0)
    return jnp.dot(keys, qtz, preferred_element_type=_F32)


def _attn_softmax(s, sink_ref, bias_ref, h, edge_ok):
    s = s + bias_ref[h]
    if edge_ok is not None:
        s = jnp.where(edge_ok, s, NEG)
    sk = sink_ref[h]
    m = jnp.maximum(jnp.max(s, axis=0, keepdims=True), sk)
    p = jnp.exp(s - m)
    den = jnp.sum(p, axis=0, keepdims=True) + jnp.exp(sk - m)
    return p.astype(_BF16), 1.0 / den


def _attn_values(p, rden, vcat, blk, h):
    g = h // GQA_GROUP
    vtg = vcat[g * HEAD_DIM:(g + 1) * HEAD_DIM, blk * BLOCK:(blk + 3) * BLOCK]
    return jnp.dot(vtg, p, preferred_element_type=_F32) * rden


def _mix_ffn_kernel(sink_ref, qt_ref, kp_ref, kc_ref, kn_ref, vp_ref, vc_ref, vn_ref, bias_ref,
                    h_ref, hf_ref, hb_ref, o_ref, gout_ref, wout_ref, g2_ref, wgu_ref, wd_ref,
                    gf_ref, y_ref, attn_ref, *, tiles_per_seq):
    j = pl.program_id(0)
    tm = h_ref.shape[0]
    nblk = tm // BLOCK

    @pl.when(j == 0)
    def _():
        attn_ref[...] = jnp.zeros_like(attn_ref)

    pos = jnp.minimum(j, pl.num_programs(0) - 2) % tiles_per_seq
    kcat = jnp.concatenate([kp_ref[...], kc_ref[...], kn_ref[...]], axis=0)
    vcat = jnp.concatenate([vp_ref[...], vc_ref[...], vn_ref[...]], axis=1)
    krow = lax.broadcasted_iota(jnp.int32, (3 * BLOCK, BLOCK), 0)
    edge = {0: (krow >= BLOCK) | (pos > 0)}
    ok_next = (krow < 2 * BLOCK) | (pos < tiles_per_seq - 1)
    edge[nblk - 1] = ok_next if nblk > 1 else edge[0] & ok_next
    units = [(blk, h) for blk in range(nblk) for h in range(N_ATTN_HEADS)]
    n_slices = len(FF_SLICES)
    per = -(-len(units) // n_slices)
    groups = [units[i * per:(i + 1) * per] for i in range(n_slices)]
    attn_out = {}
    scored = [_attn_scores(qt_ref, kcat, blk, hh) for blk, hh in groups[0]]
    weighted = []

    lane = lax.broadcasted_iota(jnp.int32, (tm, LANES), 1)
    lo_half = lane < HEAD_DIM
    hm_tiles = []
    for p in range(N_MLSTM_HEADS // 2):
        sl = slice(p * LANES, (p + 1) * LANES)
        hm = (jax.nn.sigmoid(o_ref[:, sl].astype(_F32))
              * (hf_ref[:, sl].astype(_F32) + hb_ref[:, sl].astype(_F32)))
        sq = hm * hm
        ss_lo = jnp.sum(jnp.where(lo_half, sq, 0.0), axis=1, keepdims=True)
        ss_hi = jnp.sum(jnp.where(lo_half, 0.0, sq), axis=1, keepdims=True)
        ms = jnp.where(lo_half, ss_lo, ss_hi) * (1.0 / HEAD_DIM)
        hm_tiles.append((hm * lax.rsqrt(ms + EPS) * gout_ref[:, sl]).astype(_BF16))
    hm_all = jnp.concatenate(hm_tiles, axis=1)
    h = (h_ref[...]
         + jnp.dot(attn_ref[...], wout_ref[0:ATTN_WIDTH, :], preferred_element_type=_F32)
         + jnp.dot(hm_all, wout_ref[ATTN_WIDTH:MIX_WIDTH, :], preferred_element_type=_F32))
    hn = _rmsnorm(h, g2_ref[...]).astype(_BF16)
    acc = None
    for t, (lo, hi) in enumerate(FF_SLICES):
        upcoming = ([_attn_scores(qt_ref, kcat, blk, hh) for blk, hh in groups[t + 1]]
                    if t + 1 < n_slices else [])
        g = jnp.dot(hn, wgu_ref[:, lo:hi], preferred_element_type=_F32)
        u = jnp.dot(hn, wgu_ref[:, D_FF + lo:D_FF + hi], preferred_element_type=_F32)
        for (blk, hh), p, rden in weighted:
            attn_out[blk, hh] = _attn_values(p, rden, vcat, blk, hh)
        weighted = [((blk, hh),) + _attn_softmax(s, sink_ref, bias_ref, hh, edge.get(blk))
                    for (blk, hh), s in zip(groups[t], scored)]
        act = (jax.nn.silu(g) * u).astype(_BF16)
        part = jnp.dot(act, wd_ref[lo:hi, :], preferred_element_type=_F32)
        acc = part if acc is None else acc + part
        scored = upcoming
    for (blk, hh), p, rden in weighted:
        attn_out[blk, hh] = _attn_values(p, rden, vcat, blk, hh)
    h = h + 0.5 * acc
    y_ref[...] = _rmsnorm(h, gf_ref[...])

    for blk in range(nblk):
        heads = [attn_out[blk, hh] for hh in range(N_ATTN_HEADS)]
        attn_ref[blk * BLOCK:(blk + 1) * BLOCK, :] = (
            jnp.concatenate(heads, axis=0).T.astype(_BF16))


def _mix_ffn(seq, pn2, pt3, h2, hf2, hb2, sink, bias_t, gout, wout, g2, wgu, wd, gf):
    n = h2.shape[0]
    tm = TOKEN_TILE
    nt = n // tm
    per_seq = seq // tm
    per_blk = tm // BLOCK
    nblocks = n // BLOCK
    blk_per_seq = seq // BLOCK
    ta = lambda j: jnp.minimum(j, nt - 1)
    tf = lambda j: jnp.maximum(j - 1, 0)
    kcol = PN_K_A // KV_WIDTH
    vrow = PT_V_A // KV_WIDTH
    kprev = lambda j: jnp.maximum(ta(j) * per_blk - 1, 0)
    knext = lambda j: jnp.minimum((ta(j) + 1) * per_blk, nblocks - 1)
    vmain = lambda j: (ta(j) // per_seq, vrow, ta(j) % per_seq)
    vprev = lambda j: (kprev(j) // blk_per_seq, vrow, kprev(j) % blk_per_seq)
    vnext = lambda j: (knext(j) // blk_per_seq, vrow, knext(j) % blk_per_seq)
    row = lambda w, c=0: pl.BlockSpec((tm, w), lambda j: (tf(j), c))
    return pl.pallas_call(
        functools.partial(_mix_ffn_kernel, tiles_per_seq=per_seq),
        grid=(nt + 1,),
        in_specs=[pl.BlockSpec(memory_space=pltpu.SMEM),
                  pl.BlockSpec((None, ATTN_WIDTH, tm),
                               lambda j: (ta(j) // per_seq, PT_Q_A // ATTN_WIDTH, ta(j) % per_seq)),
                  pl.BlockSpec((BLOCK, KV_WIDTH), lambda j: (kprev(j), kcol)),
                  pl.BlockSpec((tm, KV_WIDTH), lambda j: (ta(j), kcol)),
                  pl.BlockSpec((BLOCK, KV_WIDTH), lambda j: (knext(j), kcol)),
                  pl.BlockSpec((None, KV_WIDTH, BLOCK), vprev),
                  pl.BlockSpec((None, KV_WIDTH, tm), vmain),
                  pl.BlockSpec((None, KV_WIDTH, BLOCK), vnext),
                  _const_spec((N_ATTN_HEADS, 3 * BLOCK, BLOCK)),
                  row(D_MODEL), row(MLSTM_WIDTH), row(MLSTM_WIDTH),
                  row(MLSTM_WIDTH, PN_O_M // MLSTM_WIDTH),
                  _const_spec((1, MLSTM_WIDTH)), _const_spec((MIX_WIDTH, D_MODEL)),
                  _const_spec((1, D_MODEL)), _const_spec((D_MODEL, 2 * D_FF)),
                  _const_spec((D_FF, D_MODEL)), _const_spec((1, D_MODEL))],
        out_specs=row(D_MODEL),
        out_shape=jax.ShapeDtypeStruct((n, D_MODEL), _F32),
        scratch_shapes=[pltpu.VMEM((tm, ATTN_WIDTH), _BF16)],
        compiler_params=pltpu.CompilerParams(dimension_semantics=("arbitrary",),
                                             vmem_limit_bytes=VMEM_LIMIT),
        name="attn_out_ffn2",
    )(sink, pt3, pn2, pn2, pn2, pt3, pt3, pt3, bias_t, h2, hf2, hb2, pn2,
      gout, wout, g2, wgu, wd, gf)


def _chunk_rows(d, st, m_prev):
    nh = N_MLSTM_HEADS
    a = st[ST_A:ST_A + nh]
    gl = st[ST_GL:ST_GL + nh]
    a_end = st[ST_AEND:ST_AEND + nh]
    m_loc = st[ST_MLOC:ST_MLOC + nh]
    g = jnp.maximum(m_prev, gl)
    m_new = jnp.maximum(a_end + m_prev, m_loc)
    row8 = lax.broadcasted_iota(jnp.int32, (nh, CHUNK), 0)
    zero8 = jnp.zeros((nh, CHUNK), _F32)
    zero_half = jnp.zeros((DEC_K, CHUNK), _F32)
    decay_rhs = []
    for h in range(nh):
        hot = row8 == h
        one = jnp.where(hot, 1.0, 0.0)
        blk = jnp.concatenate([one, one, one] + [jnp.where(hot, t, 0.0) for t in _split3(-g * LOG2E)]
                              + [zero8, zero8], axis=0)
        x = jnp.concatenate([zero_half, blk] if d else [blk, zero_half], axis=0)
        decay_rhs.append(x.astype(_BF16))
    return dict(e=st[ST_E:ST_E + nh], decay_rhs=decay_rhs,
                inter=jnp.exp(m_prev - g),
                floor=jnp.exp(-(a + g)),
                m_new=m_new,
                s_old=jnp.exp(a_end + m_prev - m_new), s_loc=jnp.exp(m_loc - m_new))


def _chunk_input_matmuls(d, rows, k, qt, vt, dec):
    nh = N_MLSTM_HEADS
    zero_head = jnp.zeros((HEAD_DIM, CHUNK), _BF16)
    e = rows["e"]
    heads = []
    for h in range(nh):
        p, par = divmod(h, 2)
        hs = slice(h * HEAD_DIM, (h + 1) * HEAD_DIM)
        qt_h = qt[hs, :]
        qtz = jnp.concatenate([zero_head, qt_h] if par else [qt_h, zero_head], axis=0)
        k_pair = k[:, p * LANES:(p + 1) * LANES]
        vt_h = vt[hs, :]
        heads.append(dict(
            qtz=qtz, vt=vt_h,
            dexp=jnp.dot(dec, rows["decay_rhs"][h], preferred_element_type=_F32),
            c_loc=jnp.dot((vt_h.astype(_F32) * e[h:h + 1, :]).astype(_BF16), k_pair,
                          preferred_element_type=_F32)))
    n_loc = jnp.dot(e.astype(_BF16), k, preferred_element_type=_F32)
    return heads, n_loc


def _chunk_state_matmuls(heads, qt, c_state, n_state):
    for hd, cz in zip(heads, c_state):
        hd["nd_state"] = jnp.dot(cz.astype(_BF16), hd["qtz"], preferred_element_type=_F32)
    return jnp.dot(n_state.astype(_BF16), qt, preferred_element_type=_F32)


def _state_update(rows, heads, n_loc, c_state, n_state):
    nh = N_MLSTM_HEADS
    s_old, s_loc = rows["s_old"], rows["s_loc"]
    c_new = [s_old[h:h + 1, :] * cz + s_loc[h:h + 1, :] * hd["c_loc"]
             for h, (hd, cz) in enumerate(zip(heads, c_state))]
    rep = MLSTM_WIDTH // LANES
    head_of_lane = lax.broadcasted_iota(jnp.int32, (nh, MLSTM_WIDTH), 1) // HEAD_DIM
    own = head_of_lane == lax.broadcasted_iota(jnp.int32, (nh, MLSTM_WIDTH), 0)
    n_new = (jnp.concatenate([s_old] * rep, axis=1) * n_state
             + jnp.concatenate([s_loc] * rep, axis=1) * jnp.where(own, n_loc, 0.0))
    return c_new, n_new


def _chunk_weighted_values(d, heads, s_ref, i):
    row = lax.broadcasted_iota(jnp.int32, (CHUNK, CHUNK), 0)
    lane = lax.broadcasted_iota(jnp.int32, (CHUNK, CHUNK), 1)
    keep = (row >= lane) if d else (row <= lane)
    for h, hd in enumerate(heads):
        qk = s_ref[i, h].astype(_F32) * jnp.exp2(jnp.where(keep, hd["dexp"], NEG))
        hd["den_intra"] = jnp.sum(qk, axis=0, keepdims=True)
        hd["nd_intra"] = jnp.dot(hd["vt"], qk.astype(_BF16), preferred_element_type=_F32)


def _chunk_finish(rows, heads, den_state, out_ref, rs):
    inter, floor = rows["inter"], rows["floor"]
    for p in range(N_MLSTM_HEADS // 2):
        tiles = []
        for h in (2 * p, 2 * p + 1):
            hd = heads[h]
            it = inter[h:h + 1, :]
            den = hd["den_intra"] + it * den_state[h:h + 1, :]
            num = hd["nd_intra"] + it * hd["nd_state"]
            tiles.append(num / jnp.maximum(jnp.abs(den), floor[h:h + 1, :]))
        out_ref[rs, p * LANES:(p + 1) * LANES] = jnp.concatenate(tiles, axis=0).T.astype(_BF16)


def _mlstm_seq_kernel(kf_ref, qtf_ref, vtf_ref, decf_ref, stf_ref, sf_ref,
                      kb_ref, qtb_ref, vtb_ref, decb_ref, stb_ref, sb_ref,
                      hf_ref, hb_ref, c_ref, n_ref, m_ref):
    @pl.when(pl.program_id(1) == 0)
    def _():
        c_ref[...] = jnp.zeros_like(c_ref)
        n_ref[...] = jnp.zeros_like(n_ref)
        m_ref[...] = jnp.full_like(m_ref, NEG)

    nh = N_MLSTM_HEADS
    n_sub = kf_ref.shape[0] // CHUNK
    refs = [(kf_ref, qtf_ref, vtf_ref, decf_ref, stf_ref, hf_ref),
            (kb_ref, qtb_ref, vtb_ref, decb_ref, stb_ref, hb_ref)]
    score_refs = [sf_ref, sb_ref]
    order = [list(range(n_sub)), list(range(n_sub - 1, -1, -1))]
    c_state = [[c_ref[d, h] for h in range(nh)] for d in range(2)]
    n_state = [n_ref[d] for d in range(2)]
    rows = [[], []]
    for d in range(2):
        m = m_ref[d]
        for i in order[d]:
            rows[d].append(_chunk_rows(d, refs[d][4][:, i * CHUNK:(i + 1) * CHUNK], m))
            m = rows[d][-1]["m_new"]
        m_ref[d] = m

    work = [[None] * n_sub, [None] * n_sub]

    def start(step):
        for d in range(2):
            k_ref, qt_ref, vt_ref, dec_ref, _, _ = refs[d]
            ts = slice(order[d][step] * CHUNK, (order[d][step] + 1) * CHUNK)
            heads, n_loc = _chunk_input_matmuls(d, rows[d][step], k_ref[ts, :], qt_ref[:, ts],
                                                vt_ref[:, ts], dec_ref[ts, :])
            work[d][step] = dict(heads=heads, n_loc=n_loc, ts=ts)
        for d in range(2):
            w = work[d][step]
            w["den_state"] = _chunk_state_matmuls(w["heads"], refs[d][1][:, w["ts"]],
                                                  c_state[d], n_state[d])
        for d in range(2):
            w = work[d][step]
            c_state[d], n_state[d] = _state_update(rows[d][step], w["heads"], w["n_loc"],
                                                   c_state[d], n_state[d])

    def weigh(step):
        for d in range(2):
            _chunk_weighted_values(d, work[d][step]["heads"], score_refs[d], order[d][step])

    def finish(step):
        for d in range(2):
            w = work[d][step]
            _chunk_finish(rows[d][step], w["heads"], w["den_state"], refs[d][5], w["ts"])

    for step in range(n_sub + 2):
        if step < n_sub:
            start(step)
        if 1 <= step <= n_sub:
            weigh(step - 1)
        if step >= 2:
            finish(step - 2)
    for d in range(2):
        for h in range(nh):
            c_ref[d, h] = c_state[d][h]
        n_ref[d] = n_state[d]


def _mlstm_seq(k3, qt3, pt3, dec3, st3, sc5):
    b, s, _ = k3.shape
    sub = SEQ_CHUNKS if (s // CHUNK) % SEQ_CHUNKS == 0 else 1
    blk = sub * CHUNK
    nc = s // blk
    vrow = PT_V_M // MLSTM_WIDTH

    def specs(pos, d):
        return [pl.BlockSpec((None, blk, MLSTM_WIDTH), lambda bi, c: (bi, pos(c), 0)),
                pl.BlockSpec((None, MLSTM_WIDTH, blk), lambda bi, c: (bi, 0, pos(c))),
                pl.BlockSpec((None, MLSTM_WIDTH, blk), lambda bi, c: (bi, vrow, pos(c))),
                pl.BlockSpec((None, blk, LANES), lambda bi, c: (bi, pos(c), 0)),
                pl.BlockSpec((None, ST_ROWS, blk), lambda bi, c: (bi, d, pos(c))),
                pl.BlockSpec((None, sub, N_MLSTM_HEADS, CHUNK, CHUNK),
                             lambda bi, c: (bi, pos(c), 0, 0, 0))]

    fwd = lambda c: c
    bwd = lambda c: nc - 1 - c
    out = lambda pos: pl.BlockSpec((None, blk, MLSTM_WIDTH), lambda bi, c: (bi, pos(c), 0))
    args = (k3, qt3, pt3, dec3, st3, sc5)
    return pl.pallas_call(
        _mlstm_seq_kernel,
        grid=(b, nc),
        in_specs=specs(fwd, 0) + specs(bwd, 1),
        out_specs=[out(fwd), out(bwd)],
        out_shape=[jax.ShapeDtypeStruct((b, s, MLSTM_WIDTH), _BF16)] * 2,
        scratch_shapes=[pltpu.VMEM((2, N_MLSTM_HEADS, HEAD_DIM, LANES), _F32),
                        pltpu.VMEM((2, N_MLSTM_HEADS, MLSTM_WIDTH), _F32),
                        pltpu.VMEM((2, N_MLSTM_HEADS, LANES), _F32)],
        compiler_params=pltpu.CompilerParams(dimension_semantics=("parallel", "arbitrary")),
        name="mlstm_seq",
    )(*args, *args)


def _t5_bucket(rel):
    nb = N_BUCKETS // 2
    ret = (rel > 0).astype(np.int32) * nb
    n = np.abs(rel)
    max_exact = nb // 2
    large = max_exact + (np.log(np.maximum(n, 1) / max_exact)
                         / math.log(MAX_DISTANCE / max_exact) * (nb - max_exact)).astype(np.int32)
    large = np.minimum(large, nb - 1)
    return (ret + np.where(n < max_exact, n, large)).astype(np.int32)


def _prepare_params(g_ffn1, w_ffn1_gu, w_ffn1_down, g_mix, w_in, w_conv, b_gates, attn_sink,
                    g_mlstm_out, w_out, g_ffn2, w_ffn2_gu, w_ffn2_down, rel_table, g_final):
    row = lambda g: g.reshape(1, -1).astype(_F32)
    offs = np.cumsum((0,) + SPLIT_SIZES)
    q_a, k_a, v_a, q_m, k_m, v_m, o_m, gate = (w_in[0][:, offs[i]:offs[i + 1]] for i in range(8))
    wqk = jnp.concatenate([q_m, k_m], axis=1).astype(_BF16)
    wn = jnp.concatenate([o_m, k_a], axis=1).astype(_BF16)
    gate_pad = jnp.pad(gate, ((0, 0), (0, GATE_ROWS_PAD - N_GATES)))
    wt = jnp.concatenate([q_a, v_m, v_a, gate_pad], axis=1).T.astype(_BF16)
    bgate = b_gates[0].reshape(N_GATES, 1).astype(_F32)
    wconv = jnp.pad(w_conv[0].astype(_F32), ((0, SUBLANES - CONV_WIDTH), (0, 0)))
    kj = np.arange(3 * BLOCK)[:, None]
    qi = np.arange(BLOCK)[None, :]
    bucket = jnp.asarray(_t5_bucket((kj - BLOCK) - qi).reshape(-1))
    onehot = (bucket[None, :] == jnp.arange(N_BUCKETS)[:, None]).astype(_F32)
    bias_t = jnp.dot(rel_table.astype(_F32).T, onehot, precision=lax.Precision.HIGHEST)
    bias_t = bias_t.reshape(N_ATTN_HEADS, 3 * BLOCK, BLOCK)
    window = np.abs((kj - BLOCK) - qi) <= WINDOW
    bias_t = jnp.where(jnp.asarray(window)[None], bias_t, NEG)
    return dict(
        g1=row(g_ffn1[0]), wgu1=w_ffn1_gu[0].astype(_BF16), wd1=w_ffn1_down[0].astype(_BF16),
        gmix=row(g_mix[0]), wqk=wqk, wn=wn, wt=wt, bgate=bgate, wconv=wconv,
        sink=attn_sink[0].astype(_F32), bias_t=bias_t, gout=row(g_mlstm_out[0]),
        wout=w_out[0].astype(_BF16), g2=row(g_ffn2[0]), wgu2=w_ffn2_gu[0].astype(_BF16),
        wd2=w_ffn2_down[0].astype(_BF16), gf=row(g_final))


def _trunk(x, p):
    b, s, d = x.shape
    n = b * s
    h2, pn2, pt3, k2, qt3, st3, dec2, sc4 = _ffn_proj(
        x.reshape(n, d), s, p["g1"], p["wgu1"], p["wd1"], p["gmix"], p["wqk"], p["wn"], p["wt"],
        p["bgate"], p["wconv"])
    hf3, hb3 = _mlstm_seq(k2.reshape(b, s, MLSTM_WIDTH), qt3, pt3, dec2.reshape(b, s, LANES), st3,
                          sc4.reshape(b, s // CHUNK, N_MLSTM_HEADS, CHUNK, CHUNK))
    y2 = _mix_ffn(s, pn2, pt3, h2, hf3.reshape(n, MLSTM_WIDTH), hb3.reshape(n, MLSTM_WIDTH),
                  p["sink"], p["bias_t"], p["gout"], p["wout"], p["g2"], p["wgu2"], p["wd2"],
                  p["gf"])
    return y2.reshape(b, s, d)


def kernel(x_prompt, x_sample, g_ffn1, w_ffn1_gu, w_ffn1_down, g_mix, w_in, w_conv, b_gates,
           attn_sink, g_mlstm_out, w_out, g_ffn2, w_ffn2_gu, w_ffn2_down, rel_bias_table, g_final):
    p = _prepare_params(g_ffn1, w_ffn1_gu, w_ffn1_down, g_mix, w_in, w_conv, b_gates, attn_sink,
                        g_mlstm_out, w_out, g_ffn2, w_ffn2_gu, w_ffn2_down, rel_bias_table, g_final)
    return (_trunk(x_prompt, p), _trunk(x_sample, p))
```

```python
import functools
import math

import jax
import jax.numpy as jnp
import numpy as np
from jax import lax
from jax.experimental import pallas as pl
from jax.experimental.pallas import tpu as pltpu

D_MODEL = 1024
HEAD_DIM = 64
N_ATTN_HEADS = 8
N_KV_HEADS = 2
GQA_GROUP = N_ATTN_HEADS // N_KV_HEADS
ATTN_WIDTH = N_ATTN_HEADS * HEAD_DIM
KV_WIDTH = N_KV_HEADS * HEAD_DIM
N_MLSTM_HEADS = 8
MLSTM_WIDTH = N_MLSTM_HEADS * HEAD_DIM
MIX_WIDTH = ATTN_WIDTH + MLSTM_WIDTH
WINDOW = 128
BLOCK = 128
N_BUCKETS = 32
MAX_DISTANCE = 128
CHUNK = 128
CONV_WIDTH = 5
D_FF = 2816
EPS = 1e-6
NEG = -1e30
LOG2E = math.log2(math.e)
SPLIT_SIZES = (ATTN_WIDTH, KV_WIDTH, KV_WIDTH, MLSTM_WIDTH, MLSTM_WIDTH, MLSTM_WIDTH,
               MLSTM_WIDTH, 4 * N_MLSTM_HEADS)
N_GATES = 4 * N_MLSTM_HEADS

LANES = 128
SUBLANES = 8
BF16_SUBLANES = 16
MXU_DIM = 256

PT_Q_A = 0
PT_V_M = ATTN_WIDTH
PT_V_A = PT_V_M + MLSTM_WIDTH
PT_ROWS = PT_V_A + KV_WIDTH
WT_K_A = PT_ROWS
WT_GATES = WT_K_A + KV_WIDTH
WT_ROWS = WT_GATES + N_GATES

HALO = BF16_SUBLANES
TOKEN_TILE = 512
SEQ_CHUNKS = 8
FF_SLICE = MXU_DIM
FF_SLICES = tuple((lo, min(lo + FF_SLICE, D_FF)) for lo in range(0, D_FF, FF_SLICE))
VMEM_BYTES = 64 * 1024 * 1024
VMEM_LIMIT = VMEM_BYTES * 7 // 8

ST_A, ST_GL, ST_E, ST_AEND, ST_MLOC = (i * N_MLSTM_HEADS for i in range(5))
ST_ROWS = 5 * N_MLSTM_HEADS
DEC_K = LANES // 2

_F32 = jnp.float32
_BF16 = jnp.bfloat16


def _rmsnorm(x, g):
    y = x * lax.rsqrt(jnp.mean(x * x, axis=-1, keepdims=True) + EPS)
    return y * g


def _split3(x):
    hi = x.astype(_BF16).astype(_F32)
    rem = x - hi
    mid = rem.astype(_BF16).astype(_F32)
    lo = (rem - mid).astype(_BF16).astype(_F32)
    return hi, mid, lo


def _const_spec(shape):
    nd = len(shape)
    return pl.BlockSpec(shape, lambda *_: (0,) * nd, pipeline_mode=pl.Buffered(1))


def _conv_unit(qk_ref, wconv_ref, c, r0):
    cs = slice(c * LANES, (c + 1) * LANES)
    w = wconv_ref[:, cs]
    y = None
    for tap in range(CONV_WIDTH):
        lo = r0 + HALO - CONV_WIDTH // 2 + tap
        term = qk_ref[lo:lo + CHUNK, cs] * w[tap:tap + 1, :]
        y = term if y is None else y + term
    return jax.nn.silu(y)


def _gate_scan_start(gt, tri):
    nh = N_MLSTM_HEADS
    out = []
    for d in range(2):
        ig = gt[2 * d * nh:(2 * d + 1) * nh, :]
        fg = gt[(2 * d + 1) * nh:(2 * d + 2) * nh, :]
        lf = jnp.minimum(fg, 0.0) - jnp.log1p(jnp.exp(-jnp.abs(fg)))
        parts = jnp.concatenate(_split3(lf), axis=0).astype(_BF16)
        sums = jnp.dot(parts, tri[d], preferred_element_type=_F32)
        out.append((ig, sums[0:nh] + sums[nh:2 * nh] + sums[2 * nh:3 * nh]))
    return out


def _gate_scan_finish(scans, st_ref, dec_ref, cs):
    nh = N_MLSTM_HEADS
    lane8 = lax.broadcasted_iota(jnp.int32, (nh, CHUNK), 1)
    ones = jnp.ones((3 * nh, CHUNK), _F32)
    zpad = jnp.zeros((DEC_K - 6 * nh, CHUNK), _F32)
    dec_rows = []
    for d, (ig, a) in enumerate(scans):
        r = ig - a
        gl = r
        sh = 1
        while sh < CHUNK:
            if d == 0:
                gl = jnp.maximum(gl, jnp.where(lane8 >= sh, pltpu.roll(gl, sh, 1), NEG))
            else:
                gl = jnp.maximum(
                    gl, jnp.where(lane8 < CHUNK - sh, pltpu.roll(gl, CHUNK - sh, 1), NEG))
            sh *= 2
        far = CHUNK - 1 if d == 0 else 0
        a_end = jnp.broadcast_to(a[:, far:far + 1], (nh, CHUNK))
        m_loc = a_end + jnp.broadcast_to(gl[:, far:far + 1], (nh, CHUNK))
        e = jnp.exp(a_end + r - m_loc)
        st_ref[d * ST_ROWS:(d + 1) * ST_ROWS, cs] = jnp.concatenate(
            [a, gl, e, a_end, m_loc], axis=0)
        dec_rows += list(_split3(r * LOG2E)) + [ones, zpad]
    dec_ref[cs, :] = jnp.concatenate(dec_rows, axis=0).T.astype(_BF16)


def _ffn_proj_kernel(xm_ref, xn_ref, g1_ref, wgu_ref, wd_ref, gmix_ref, wqk_ref, wn_ref,
                     wt_ref, bgate_ref, wconv_ref,
                     h_ref, om_ref, pt_ref, ka_ref, k_ref, qt_ref, st_ref, dec_ref, s_ref,
                     qk_scr, gt_scr, *, tiles_per_seq):
    j = pl.program_id(0)
    tm = xm_ref.shape[0]
    n_chunks = tm // CHUNK

    @pl.when(j == 0)
    def _():
        qk_scr[...] = jnp.zeros_like(qk_scr)
        gt_scr[...] = jnp.zeros_like(gt_scr)

    srow = lax.broadcasted_iota(jnp.int32, (CHUNK, CHUNK), 0)
    slane = lax.broadcasted_iota(jnp.int32, (CHUNK, CHUNK), 1)
    tri = [(srow <= slane).astype(_BF16), (srow >= slane).astype(_BF16)]
    scans = [_gate_scan_start(gt_scr[:, c * CHUNK:(c + 1) * CHUNK], tri)
             for c in range(n_chunks)]

    n_pairs = N_MLSTM_HEADS // 2
    zero_head = jnp.zeros((HEAD_DIM, CHUNK), _BF16)

    def conv_unit(p, c):
        r0 = c * CHUNK
        qt = _conv_unit(qk_scr, wconv_ref, p, r0).T.astype(_BF16)
        k = (_conv_unit(qk_scr, wconv_ref, n_pairs + p, r0) * (HEAD_DIM ** -0.5)).astype(_BF16)
        qt_ref[p * LANES:(p + 1) * LANES, r0:r0 + CHUNK] = qt
        k_ref[r0:r0 + CHUNK, p * LANES:(p + 1) * LANES] = k
        return p, c, k, qt

    def score_unit(p, c, k, qt):
        for par in range(2):
            half = qt[par * HEAD_DIM:(par + 1) * HEAD_DIM]
            qtz = jnp.concatenate([zero_head, half] if par else [half, zero_head], axis=0)
            s_ref[c, 2 * p + par] = jnp.dot(k, qtz, preferred_element_type=_F32).astype(_BF16)

    conv_units = [functools.partial(conv_unit, p, c)
                  for c in range(n_chunks) for p in range(n_pairs)]
    scan_units = [functools.partial(_gate_scan_finish, scans[c], st_ref, dec_ref,
                                    slice(c * CHUNK, (c + 1) * CHUNK)) for c in range(n_chunks)]

    pos = jnp.minimum(j, pl.num_programs(0) - 2) % tiles_per_seq
    x = jnp.concatenate([xm_ref[...], xn_ref[...]], axis=0)
    xn = _rmsnorm(x, g1_ref[...]).astype(_BF16)
    per_slice = -(-len(conv_units) // (len(FF_SLICES) - 1))
    convolved = []
    acts = []
    for lo, hi in FF_SLICES:
        for done in convolved:
            score_unit(*done)
        g = jnp.dot(xn, wgu_ref[:, lo:hi], preferred_element_type=_F32)
        u = jnp.dot(xn, wgu_ref[:, D_FF + lo:D_FF + hi], preferred_element_type=_F32)
        convolved = [unit() for unit in conv_units[:per_slice]]
        del conv_units[:per_slice]
        if not convolved and scan_units:
            scan_units.pop(0)()
        acts.append((jax.nn.silu(g) * u).astype(_BF16))
    for done in convolved:
        score_unit(*done)
    acc = jnp.dot(jnp.concatenate(acts, axis=1), wd_ref[...], preferred_element_type=_F32)
    while scan_units:
        scan_units.pop(0)()
    h = x + 0.5 * acc
    h_ref[...] = h[0:tm]
    un = _rmsnorm(h, gmix_ref[...]).astype(_BF16)
    qk = jnp.dot(un, wqk_ref[...], preferred_element_type=_F32)
    prev_tail = qk_scr[tm:tm + HALO, :]
    qk_scr[0:HALO, :] = jnp.where(pos > 0, prev_tail, 0.0)
    qk_scr[HALO:HALO + tm, :] = qk[0:tm]
    qk_scr[HALO + tm:, :] = jnp.where(pos < tiles_per_seq - 1, qk[tm:], 0.0)
    un_c = un[0:tm]
    om_ref[...] = jnp.dot(un_c, wn_ref[...], preferred_element_type=_F32).astype(_BF16)
    t = lax.dot_general(wt_ref[...], un_c, (((1,), (1,)), ((), ())),
                        preferred_element_type=_F32)
    pt_ref[...] = t[:PT_ROWS].astype(_BF16)
    ka_ref[...] = t[WT_K_A:WT_K_A + KV_WIDTH].T.astype(_BF16)
    gt_scr[...] = t[WT_GATES:WT_GATES + N_GATES] + bgate_ref[...]


def _ffn_proj(x2, seq, g1, wgu, wd, gmix, wqk, wn, wt, bgate, wconv):
    n = x2.shape[0]
    tm = TOKEN_TILE
    nt = n // tm
    per_seq = seq // tm
    per = tm // HALO
    nhalo = n // HALO
    ta = lambda j: jnp.minimum(j, nt - 1)
    tf = lambda j: jnp.maximum(j - 1, 0)
    rowa = lambda w: pl.BlockSpec((tm, w), lambda j: (ta(j), 0))
    rowf = lambda w: pl.BlockSpec((tm, w), lambda j: (tf(j), 0))
    cola = lambda r: pl.BlockSpec((None, r, tm), lambda j: (ta(j) // per_seq, 0, ta(j) % per_seq))
    colf = lambda r: pl.BlockSpec((None, r, tm), lambda j: (tf(j) // per_seq, 0, tf(j) % per_seq))
    b = n // seq
    return pl.pallas_call(
        functools.partial(_ffn_proj_kernel, tiles_per_seq=per_seq),
        grid=(nt + 1,),
        in_specs=[rowa(D_MODEL),
                  pl.BlockSpec((HALO, D_MODEL),
                               lambda j: (jnp.minimum((ta(j) + 1) * per, nhalo - 1), 0)),
                  _const_spec((1, D_MODEL)), _const_spec((D_MODEL, 2 * D_FF)),
                  _const_spec((D_FF, D_MODEL)), _const_spec((1, D_MODEL)),
                  _const_spec((D_MODEL, 2 * MLSTM_WIDTH)),
                  _const_spec((D_MODEL, MLSTM_WIDTH)),
                  _const_spec((WT_ROWS, D_MODEL)),
                  _const_spec((N_GATES, 1)), _const_spec((SUBLANES, 2 * MLSTM_WIDTH))],
        out_specs=[rowa(D_MODEL), rowa(MLSTM_WIDTH), cola(PT_ROWS), rowa(KV_WIDTH),
                   rowf(MLSTM_WIDTH), colf(MLSTM_WIDTH), colf(2 * ST_ROWS), rowf(LANES),
                   pl.BlockSpec((tm // CHUNK, N_MLSTM_HEADS, CHUNK, CHUNK),
                                lambda j: (tf(j), 0, 0, 0))],
        out_shape=[jax.ShapeDtypeStruct((n, D_MODEL), _F32),
                   jax.ShapeDtypeStruct((n, MLSTM_WIDTH), _BF16),
                   jax.ShapeDtypeStruct((b, PT_ROWS, seq), _BF16),
                   jax.ShapeDtypeStruct((n, KV_WIDTH), _BF16),
                   jax.ShapeDtypeStruct((n, MLSTM_WIDTH), _BF16),
                   jax.ShapeDtypeStruct((b, MLSTM_WIDTH, seq), _BF16),
                   jax.ShapeDtypeStruct((b, 2 * ST_ROWS, seq), _F32),
                   jax.ShapeDtypeStruct((n, LANES), _BF16),
                   jax.ShapeDtypeStruct((n // CHUNK, N_MLSTM_HEADS, CHUNK, CHUNK), _BF16)],
        scratch_shapes=[pltpu.VMEM((tm + 2 * HALO, 2 * MLSTM_WIDTH), _F32),
                        pltpu.VMEM((N_GATES, tm), _F32)],
        compiler_params=pltpu.CompilerParams(dimension_semantics=("arbitrary",),
                                             vmem_limit_bytes=VMEM_LIMIT),
        name="ffn1_proj_prep",
    )(x2, x2, g1, wgu, wd, gmix, wqk, wn, wt, bgate, wconv)


def _attn_scores(qt_ref, kcat, blk, h):
    zeros = jnp.zeros((HEAD_DIM, BLOCK), _BF16)
    keys = kcat[blk * BLOCK:(blk + 3) * BLOCK, :]
    qt = (qt_ref[h * HEAD_DIM:(h + 1) * HEAD_DIM, blk * BLOCK:(blk + 1) * BLOCK]
          * (HEAD_DIM ** -0.5))
    qtz = jnp.concatenate([qt, zeros] if h < GQA_GROUP else [zeros, qt], axis=0)
    return jnp.dot(keys, qtz, preferred_element_type=_F32)


def _attn_softmax(s, sink_ref, bias_ref, h, edge_ok):
    s = s + bias_ref[h]
    if edge_ok is not None:
        s = jnp.where(edge_ok, s, NEG)
    sk = sink_ref[h]
    m = jnp.maximum(jnp.max(s, axis=0, keepdims=True), sk)
    p = jnp.exp(s - m)
    den = jnp.sum(p, axis=0, keepdims=True) + jnp.exp(sk - m)
    return p.astype(_BF16), 1.0 / den


def _attn_values(p, rden, vcat, blk, h):
    g = h // GQA_GROUP
    vtg = vcat[g * HEAD_DIM:(g + 1) * HEAD_DIM, blk * BLOCK:(blk + 3) * BLOCK]
    return jnp.dot(vtg, p, preferred_element_type=_F32) * rden


def _mix_ffn_kernel(sink_ref, qt_ref, kp_ref, kc_ref, kn_ref, vp_ref, vc_ref, vn_ref, bias_ref,
                    h_ref, hf_ref, hb_ref, o_ref, gout_ref, wout_ref, g2_ref, wgu_ref, wd_ref,
                    gf_ref, y_ref, attn_ref, *, tiles_per_seq):
    j = pl.program_id(0)
    tm = h_ref.shape[0]
    nblk = tm // BLOCK

    @pl.when(j == 0)
    def _():
        attn_ref[...] = jnp.zeros_like(attn_ref)

    pos = jnp.minimum(j, pl.num_programs(0) - 2) % tiles_per_seq
    kcat = jnp.concatenate([kp_ref[...], kc_ref[...], kn_ref[...]], axis=0)
    vcat = jnp.concatenate([vp_ref[...], vc_ref[...], vn_ref[...]], axis=1)
    krow = lax.broadcasted_iota(jnp.int32, (3 * BLOCK, BLOCK), 0)
    edge = {0: (krow >= BLOCK) | (pos > 0)}
    ok_next = (krow < 2 * BLOCK) | (pos < tiles_per_seq - 1)
    edge[nblk - 1] = ok_next if nblk > 1 else edge[0] & ok_next
    units = [(blk, h) for blk in range(nblk) for h in range(N_ATTN_HEADS)]
    n_slices = len(FF_SLICES)
    per = -(-len(units) // n_slices)
    groups = [units[i * per:(i + 1) * per] for i in range(n_slices)]
    attn_out = {}
    scored = [_attn_scores(qt_ref, kcat, blk, hh) for blk, hh in groups[0]]
    weighted = []

    lane = lax.broadcasted_iota(jnp.int32, (tm, LANES), 1)
    lo_half = lane < HEAD_DIM
    hm_tiles = []
    for p in range(N_MLSTM_HEADS // 2):
        sl = slice(p * LANES, (p + 1) * LANES)
        hm = (jax.nn.sigmoid(o_ref[:, sl].astype(_F32))
              * (hf_ref[:, sl].astype(_F32) + hb_ref[:, sl].astype(_F32)))
        sq = hm * hm
        ss_lo = jnp.sum(jnp.where(lo_half, sq, 0.0), axis=1, keepdims=True)
        ss_hi = jnp.sum(jnp.where(lo_half, 0.0, sq), axis=1, keepdims=True)
        ms = jnp.where(lo_half, ss_lo, ss_hi) * (1.0 / HEAD_DIM)
        hm_tiles.append((hm * lax.rsqrt(ms + EPS) * gout_ref[:, sl]).astype(_BF16))
    hm_all = jnp.concatenate(hm_tiles, axis=1)
    h = (h_ref[...]
         + jnp.dot(attn_ref[...], wout_ref[0:ATTN_WIDTH, :], preferred_element_type=_F32)
         + jnp.dot(hm_all, wout_ref[ATTN_WIDTH:MIX_WIDTH, :], preferred_element_type=_F32))
    hn = _rmsnorm(h, g2_ref[...]).astype(_BF16)
    acts = []
    for t, (lo, hi) in enumerate(FF_SLICES):
        upcoming = ([_attn_scores(qt_ref, kcat, blk, hh) for blk, hh in groups[t + 1]]
                    if t + 1 < n_slices else [])
        g = jnp.dot(hn, wgu_ref[:, lo:hi], preferred_element_type=_F32)
        u = jnp.dot(hn, wgu_ref[:, D_FF + lo:D_FF + hi], preferred_element_type=_F32)
        for (blk, hh), p, rden in weighted:
            attn_out[blk, hh] = _attn_values(p, rden, vcat, blk, hh)
        weighted = [((blk, hh),) + _attn_softmax(s, sink_ref, bias_ref, hh, edge.get(blk))
                    for (blk, hh), s in zip(groups[t], scored)]
        acts.append((jax.nn.silu(g) * u).astype(_BF16))
        scored = upcoming
    for (blk, hh), p, rden in weighted:
        attn_out[blk, hh] = _attn_values(p, rden, vcat, blk, hh)
    acc = jnp.dot(jnp.concatenate(acts, axis=1), wd_ref[...], preferred_element_type=_F32)
    h = h + 0.5 * acc
    y_ref[...] = _rmsnorm(h, gf_ref[...])

    for blk in range(nblk):
        heads = [attn_out[blk, hh] for hh in range(N_ATTN_HEADS)]
        attn_ref[blk * BLOCK:(blk + 1) * BLOCK, :] = (
            jnp.concatenate(heads, axis=0).T.astype(_BF16))


def _mix_ffn(seq, om2, ka2, pt3, h2, hf2, hb2, sink, bias_t, gout, wout, g2, wgu, wd, gf):
    n = h2.shape[0]
    tm = TOKEN_TILE
    nt = n // tm
    per_seq = seq // tm
    per_blk = tm // BLOCK
    nblocks = n // BLOCK
    blk_per_seq = seq // BLOCK
    ta = lambda j: jnp.minimum(j, nt - 1)
    tf = lambda j: jnp.maximum(j - 1, 0)
    vrow = PT_V_A // KV_WIDTH
    kprev = lambda j: jnp.maximum(ta(j) * per_blk - 1, 0)
    knext = lambda j: jnp.minimum((ta(j) + 1) * per_blk, nblocks - 1)
    vmain = lambda j: (ta(j) // per_seq, vrow, ta(j) % per_seq)
    vprev = lambda j: (kprev(j) // blk_per_seq, vrow, kprev(j) % blk_per_seq)
    vnext = lambda j: (knext(j) // blk_per_seq, vrow, knext(j) % blk_per_seq)
    row = lambda w, c=0: pl.BlockSpec((tm, w), lambda j: (tf(j), c))
    return pl.pallas_call(
        functools.partial(_mix_ffn_kernel, tiles_per_seq=per_seq),
        grid=(nt + 1,),
        in_specs=[pl.BlockSpec(memory_space=pltpu.SMEM),
                  pl.BlockSpec((None, ATTN_WIDTH, tm),
                               lambda j: (ta(j) // per_seq, PT_Q_A // ATTN_WIDTH, ta(j) % per_seq)),
                  pl.BlockSpec((BLOCK, KV_WIDTH), lambda j: (kprev(j), 0)),
                  pl.BlockSpec((tm, KV_WIDTH), lambda j: (ta(j), 0)),
                  pl.BlockSpec((BLOCK, KV_WIDTH), lambda j: (knext(j), 0)),
                  pl.BlockSpec((None, KV_WIDTH, BLOCK), vprev),
                  pl.BlockSpec((None, KV_WIDTH, tm), vmain),
                  pl.BlockSpec((None, KV_WIDTH, BLOCK), vnext),
                  _const_spec((N_ATTN_HEADS, 3 * BLOCK, BLOCK)),
                  row(D_MODEL), row(MLSTM_WIDTH), row(MLSTM_WIDTH),
                  row(MLSTM_WIDTH),
                  _const_spec((1, MLSTM_WIDTH)), _const_spec((MIX_WIDTH, D_MODEL)),
                  _const_spec((1, D_MODEL)), _const_spec((D_MODEL, 2 * D_FF)),
                  _const_spec((D_FF, D_MODEL)), _const_spec((1, D_MODEL))],
        out_specs=row(D_MODEL),
        out_shape=jax.ShapeDtypeStruct((n, D_MODEL), _F32),
        scratch_shapes=[pltpu.VMEM((tm, ATTN_WIDTH), _BF16)],
        compiler_params=pltpu.CompilerParams(dimension_semantics=("arbitrary",),
                                             vmem_limit_bytes=VMEM_LIMIT),
        name="attn_out_ffn2",
    )(sink, pt3, ka2, ka2, ka2, pt3, pt3, pt3, bias_t, h2, hf2, hb2, om2,
      gout, wout, g2, wgu, wd, gf)


def _chunk_rows(d, st, m_prev):
    nh = N_MLSTM_HEADS
    a = st[ST_A:ST_A + nh]
    gl = st[ST_GL:ST_GL + nh]
    a_end = st[ST_AEND:ST_AEND + nh]
    m_loc = st[ST_MLOC:ST_MLOC + nh]
    g = jnp.maximum(m_prev, gl)
    m_new = jnp.maximum(a_end + m_prev, m_loc)
    row8 = lax.broadcasted_iota(jnp.int32, (nh, CHUNK), 0)
    zero8 = jnp.zeros((nh, CHUNK), _F32)
    zero_half = jnp.zeros((DEC_K, CHUNK), _F32)
    decay_rhs = []
    for h in range(nh):
        hot = row8 == h
        one = jnp.where(hot, 1.0, 0.0)
        blk = jnp.concatenate([one, one, one] + [jnp.where(hot, t, 0.0) for t in _split3(-g * LOG2E)]
                              + [zero8, zero8], axis=0)
        x = jnp.concatenate([zero_half, blk] if d else [blk, zero_half], axis=0)
        decay_rhs.append(x.astype(_BF16))
    return dict(e=st[ST_E:ST_E + nh], decay_rhs=decay_rhs,
                inter=jnp.exp(m_prev - g),
                floor=jnp.exp(-(a + g)),
                m_new=m_new,
                s_old=jnp.exp(a_end + m_prev - m_new), s_loc=jnp.exp(m_loc - m_new))


def _chunk_input_matmuls(d, rows, k, qt, vt, dec):
    nh = N_MLSTM_HEADS
    zero_head = jnp.zeros((HEAD_DIM, CHUNK), _BF16)
    e = rows["e"]
    heads = []
    for h in range(nh):
        p, par = divmod(h, 2)
        hs = slice(h * HEAD_DIM, (h + 1) * HEAD_DIM)
        qt_h = qt[hs, :]
        qtz = jnp.concatenate([zero_head, qt_h] if par else [qt_h, zero_head], axis=0)
        k_pair = k[:, p * LANES:(p + 1) * LANES]
        vt_h = vt[hs, :]
        heads.append(dict(
            qtz=qtz, vt=vt_h,
            dexp=jnp.dot(dec, rows["decay_rhs"][h], preferred_element_type=_F32),
            c_loc=jnp.dot((vt_h.astype(_F32) * e[h:h + 1, :]).astype(_BF16), k_pair,
                          preferred_element_type=_F32)))
    n_loc = jnp.dot(e.astype(_BF16), k, preferred_element_type=_F32)
    return heads, n_loc


def _chunk_state_matmuls(heads, qt, c_state, n_state):
    for hd, cz in zip(heads, c_state):
        hd["nd_state"] = jnp.dot(cz.astype(_BF16), hd["qtz"], preferred_element_type=_F32)
    return jnp.dot(n_state.astype(_BF16), qt, preferred_element_type=_F32)


def _state_update(rows, heads, n_loc, c_state, n_state):
    nh = N_MLSTM_HEADS
    s_old, s_loc = rows["s_old"], rows["s_loc"]
    c_new = [s_old[h:h + 1, :] * cz + s_loc[h:h + 1, :] * hd["c_loc"]
             for h, (hd, cz) in enumerate(zip(heads, c_state))]
    rep = MLSTM_WIDTH // LANES
    head_of_lane = lax.broadcasted_iota(jnp.int32, (nh, MLSTM_WIDTH), 1) // HEAD_DIM
    own = head_of_lane == lax.broadcasted_iota(jnp.int32, (nh, MLSTM_WIDTH), 0)
    n_new = (jnp.concatenate([s_old] * rep, axis=1) * n_state
             + jnp.concatenate([s_loc] * rep, axis=1) * jnp.where(own, n_loc, 0.0))
    return c_new, n_new


def _chunk_weighted_values(d, heads, s_ref, i):
    row = lax.broadcasted_iota(jnp.int32, (CHUNK, CHUNK), 0)
    lane = lax.broadcasted_iota(jnp.int32, (CHUNK, CHUNK), 1)
    keep = (row >= lane) if d else (row <= lane)
    for h, hd in enumerate(heads):
        qk = s_ref[i, h].astype(_F32) * jnp.exp2(jnp.where(keep, hd["dexp"], NEG))
        hd["den_intra"] = jnp.sum(qk, axis=0, keepdims=True)
        hd["nd_intra"] = jnp.dot(hd["vt"], qk.astype(_BF16), preferred_element_type=_F32)


def _chunk_finish(rows, heads, den_state, out_ref, rs):
    inter, floor = rows["inter"], rows["floor"]
    for p in range(N_MLSTM_HEADS // 2):
        tiles = []
        for h in (2 * p, 2 * p + 1):
            hd = heads[h]
            it = inter[h:h + 1, :]
            den = hd["den_intra"] + it * den_state[h:h + 1, :]
            num = hd["nd_intra"] + it * hd["nd_state"]
            tiles.append(num / jnp.maximum(jnp.abs(den), floor[h:h + 1, :]))
        out_ref[rs, p * LANES:(p + 1) * LANES] = jnp.concatenate(tiles, axis=0).T.astype(_BF16)


def _mlstm_seq_kernel(kf_ref, qtf_ref, vtf_ref, decf_ref, stf_ref, sf_ref,
                      kb_ref, qtb_ref, vtb_ref, decb_ref, stb_ref, sb_ref,
                      hf_ref, hb_ref, c_ref, n_ref, m_ref):
    @pl.when(pl.program_id(1) == 0)
    def _():
        c_ref[...] = jnp.zeros_like(c_ref)
        n_ref[...] = jnp.zeros_like(n_ref)
        m_ref[...] = jnp.full_like(m_ref, NEG)

    nh = N_MLSTM_HEADS
    n_sub = kf_ref.shape[0] // CHUNK
    refs = [(kf_ref, qtf_ref, vtf_ref, decf_ref, stf_ref, hf_ref),
            (kb_ref, qtb_ref, vtb_ref, decb_ref, stb_ref, hb_ref)]
    score_refs = [sf_ref, sb_ref]
    order = [list(range(n_sub)), list(range(n_sub - 1, -1, -1))]
    c_state = [[c_ref[d, h] for h in range(nh)] for d in range(2)]
    n_state = [n_ref[d] for d in range(2)]
    rows = [[], []]
    for d in range(2):
        m = m_ref[d]
        for i in order[d]:
            rows[d].append(_chunk_rows(d, refs[d][4][:, i * CHUNK:(i + 1) * CHUNK], m))
            m = rows[d][-1]["m_new"]
        m_ref[d] = m

    work = [[None] * n_sub, [None] * n_sub]

    def start(step):
        for d in range(2):
            k_ref, qt_ref, vt_ref, dec_ref, _, _ = refs[d]
            ts = slice(order[d][step] * CHUNK, (order[d][step] + 1) * CHUNK)
            heads, n_loc = _chunk_input_matmuls(d, rows[d][step], k_ref[ts, :], qt_ref[:, ts],
                                                vt_ref[:, ts], dec_ref[ts, :])
            work[d][step] = dict(heads=heads, n_loc=n_loc, ts=ts)
        for d in range(2):
            w = work[d][step]
            w["den_state"] = _chunk_state_matmuls(w["heads"], refs[d][1][:, w["ts"]],
                                                  c_state[d], n_state[d])
        for d in range(2):
            w = work[d][step]
            c_state[d], n_state[d] = _state_update(rows[d][step], w["heads"], w["n_loc"],
                                                   c_state[d], n_state[d])

    def weigh(step):
        for d in range(2):
            _chunk_weighted_values(d, work[d][step]["heads"], score_refs[d], order[d][step])

    def finish(step):
        for d in range(2):
            w = work[d][step]
            _chunk_finish(rows[d][step], w["heads"], w["den_state"], refs[d][5], w["ts"])

    for step in range(n_sub + 2):
        if step < n_sub:
            start(step)
        if 1 <= step <= n_sub:
            weigh(step - 1)
        if step >= 2:
            finish(step - 2)
    for d in range(2):
        for h in range(nh):
            c_ref[d, h] = c_state[d][h]
        n_ref[d] = n_state[d]


def _mlstm_seq(k3, qt3, pt3, dec3, st3, sc5):
    b, s, _ = k3.shape
    sub = SEQ_CHUNKS if (s // CHUNK) % SEQ_CHUNKS == 0 else 1
    blk = sub * CHUNK
    nc = s // blk
    vrow = PT_V_M // MLSTM_WIDTH

    def specs(pos, d):
        return [pl.BlockSpec((None, blk, MLSTM_WIDTH), lambda bi, c: (bi, pos(c), 0)),
                pl.BlockSpec((None, MLSTM_WIDTH, blk), lambda bi, c: (bi, 0, pos(c))),
                pl.BlockSpec((None, MLSTM_WIDTH, blk), lambda bi, c: (bi, vrow, pos(c))),
                pl.BlockSpec((None, blk, LANES), lambda bi, c: (bi, pos(c), 0)),
                pl.BlockSpec((None, ST_ROWS, blk), lambda bi, c: (bi, d, pos(c))),
                pl.BlockSpec((None, sub, N_MLSTM_HEADS, CHUNK, CHUNK),
                             lambda bi, c: (bi, pos(c), 0, 0, 0))]

    fwd = lambda c: c
    bwd = lambda c: nc - 1 - c
    out = lambda pos: pl.BlockSpec((None, blk, MLSTM_WIDTH), lambda bi, c: (bi, pos(c), 0))
    args = (k3, qt3, pt3, dec3, st3, sc5)
    return pl.pallas_call(
        _mlstm_seq_kernel,
        grid=(b, nc),
        in_specs=specs(fwd, 0) + specs(bwd, 1),
        out_specs=[out(fwd), out(bwd)],
        out_shape=[jax.ShapeDtypeStruct((b, s, MLSTM_WIDTH), _BF16)] * 2,
        scratch_shapes=[pltpu.VMEM((2, N_MLSTM_HEADS, HEAD_DIM, LANES), _F32),
                        pltpu.VMEM((2, N_MLSTM_HEADS, MLSTM_WIDTH), _F32),
                        pltpu.VMEM((2, N_MLSTM_HEADS, LANES), _F32)],
        compiler_params=pltpu.CompilerParams(dimension_semantics=("parallel", "arbitrary")),
        name="mlstm_seq",
    )(*args, *args)


def _t5_bucket(rel):
    nb = N_BUCKETS // 2
    ret = (rel > 0).astype(np.int32) * nb
    n = np.abs(rel)
    max_exact = nb // 2
    large = max_exact + (np.log(np.maximum(n, 1) / max_exact)
                         / math.log(MAX_DISTANCE / max_exact) * (nb - max_exact)).astype(np.int32)
    large = np.minimum(large, nb - 1)
    return (ret + np.where(n < max_exact, n, large)).astype(np.int32)


def _prepare_params(g_ffn1, w_ffn1_gu, w_ffn1_down, g_mix, w_in, w_conv, b_gates, attn_sink,
                    g_mlstm_out, w_out, g_ffn2, w_ffn2_gu, w_ffn2_down, rel_table, g_final):
    row = lambda g: g.reshape(1, -1).astype(_F32)
    offs = np.cumsum((0,) + SPLIT_SIZES)
    q_a, k_a, v_a, q_m, k_m, v_m, o_m, gate = (w_in[0][:, offs[i]:offs[i + 1]] for i in range(8))
    wqk = jnp.concatenate([q_m, k_m], axis=1).astype(_BF16)
    wn = o_m.astype(_BF16)
    wt = jnp.concatenate([q_a, v_m, v_a, k_a, gate], axis=1).T.astype(_BF16)
    bgate = b_gates[0].reshape(N_GATES, 1).astype(_F32)
    wconv = jnp.pad(w_conv[0].astype(_F32), ((0, SUBLANES - CONV_WIDTH), (0, 0)))
    kj = np.arange(3 * BLOCK)[:, None]
    qi = np.arange(BLOCK)[None, :]
    bucket = jnp.asarray(_t5_bucket((kj - BLOCK) - qi).reshape(-1))
    onehot = (bucket[None, :] == jnp.arange(N_BUCKETS)[:, None]).astype(_F32)
    bias_t = jnp.dot(rel_table.astype(_F32).T, onehot, precision=lax.Precision.HIGHEST)
    bias_t = bias_t.reshape(N_ATTN_HEADS, 3 * BLOCK, BLOCK)
    window = np.abs((kj - BLOCK) - qi) <= WINDOW
    bias_t = jnp.where(jnp.asarray(window)[None], bias_t, NEG)
    return dict(
        g1=row(g_ffn1[0]), wgu1=w_ffn1_gu[0].astype(_BF16), wd1=w_ffn1_down[0].astype(_BF16),
        gmix=row(g_mix[0]), wqk=wqk, wn=wn, wt=wt, bgate=bgate, wconv=wconv,
        sink=attn_sink[0].astype(_F32), bias_t=bias_t, gout=row(g_mlstm_out[0]),
        wout=w_out[0].astype(_BF16), g2=row(g_ffn2[0]), wgu2=w_ffn2_gu[0].astype(_BF16),
        wd2=w_ffn2_down[0].astype(_BF16), gf=row(g_final))


def _trunk(x, p):
    b, s, d = x.shape
    n = b * s
    h2, om2, pt3, ka2, k2, qt3, st3, dec2, sc4 = _ffn_proj(
        x.reshape(n, d), s, p["g1"], p["wgu1"], p["wd1"], p["gmix"], p["wqk"], p["wn"], p["wt"],
        p["bgate"], p["wconv"])
    hf3, hb3 = _mlstm_seq(k2.reshape(b, s, MLSTM_WIDTH), qt3, pt3, dec2.reshape(b, s, LANES), st3,
                          sc4.reshape(b, s // CHUNK, N_MLSTM_HEADS, CHUNK, CHUNK))
    y2 = _mix_ffn(s, om2, ka2, pt3, h2, hf3.reshape(n, MLSTM_WIDTH), hb3.reshape(n, MLSTM_WIDTH),
                  p["sink"], p["bias_t"], p["gout"], p["wout"], p["g2"], p["wgu2"], p["wd2"],
                  p["gf"])
    return y2.reshape(b, s, d)


def kernel(x_prompt, x_sample, g_ffn1, w_ffn1_gu, w_ffn1_down, g_mix, w_in, w_conv, b_gates,
           attn_sink, g_mlstm_out, w_out, g_ffn2, w_ffn2_gu, w_ffn2_down, rel_bias_table, g_final):
    p = _prepare_params(g_ffn1, w_ffn1_gu, w_ffn1_down, g_mix, w_in, w_conv, b_gates, attn_sink,
                        g_mlstm_out, w_out, g_ffn2, w_ffn2_gu, w_ffn2_down, rel_bias_table, g_final)
    return (_trunk(x_prompt, p), _trunk(x_sample, p))
```

```python
import functools
import math

import jax
import jax.numpy as jnp
import numpy as np
from jax import lax
from jax.experimental import pallas as pl
from jax.experimental.pallas import tpu as pltpu

D_MODEL = 1024
HEAD_DIM = 64
N_ATTN_HEADS = 8
N_KV_HEADS = 2
GQA_GROUP = N_ATTN_HEADS // N_KV_HEADS
ATTN_WIDTH = N_ATTN_HEADS * HEAD_DIM
KV_WIDTH = N_KV_HEADS * HEAD_DIM
N_MLSTM_HEADS = 8
MLSTM_WIDTH = N_MLSTM_HEADS * HEAD_DIM
MIX_WIDTH = ATTN_WIDTH + MLSTM_WIDTH
WINDOW = 128
BLOCK = 128
N_BUCKETS = 32
MAX_DISTANCE = 128
CHUNK = 128
CONV_WIDTH = 5
D_FF = 2816
EPS = 1e-6
NEG = -1e30
LOG2E = math.log2(math.e)
SPLIT_SIZES = (ATTN_WIDTH, KV_WIDTH, KV_WIDTH, MLSTM_WIDTH, MLSTM_WIDTH, MLSTM_WIDTH,
               MLSTM_WIDTH, 4 * N_MLSTM_HEADS)
N_GATES = 4 * N_MLSTM_HEADS

LANES = 128
SUBLANES = 8
BF16_SUBLANES = 16
MXU_DIM = 256

PN_O_M = 0
PN_K_A = PN_O_M + MLSTM_WIDTH
PN_WIDTH = PN_K_A + KV_WIDTH
PT_Q_A = 0
PT_V_M = ATTN_WIDTH
PT_V_A = PT_V_M + MLSTM_WIDTH
PT_ROWS = PT_V_A + KV_WIDTH
GATE_ROWS_PAD = N_GATES

HALO = BF16_SUBLANES
TOKEN_TILE = 512
SEQ_CHUNKS = 8
FF_SLICE = MXU_DIM
FF_SLICES = tuple((lo, min(lo + FF_SLICE, D_FF)) for lo in range(0, D_FF, FF_SLICE))
VMEM_BYTES = 64 * 1024 * 1024
VMEM_LIMIT = VMEM_BYTES * 7 // 8

ST_A, ST_GL, ST_E, ST_AEND, ST_MLOC = (i * N_MLSTM_HEADS for i in range(5))
ST_ROWS = 5 * N_MLSTM_HEADS
DEC_K = LANES // 2

_F32 = jnp.float32
_BF16 = jnp.bfloat16


def _rmsnorm(x, g):
    y = x * lax.rsqrt(jnp.mean(x * x, axis=-1, keepdims=True) + EPS)
    return y * g


def _split3(x):
    hi = x.astype(_BF16).astype(_F32)
    rem = x - hi
    mid = rem.astype(_BF16).astype(_F32)
    lo = (rem - mid).astype(_BF16).astype(_F32)
    return hi, mid, lo


def _const_spec(shape):
    nd = len(shape)
    return pl.BlockSpec(shape, lambda *_: (0,) * nd, pipeline_mode=pl.Buffered(1))


def _conv_unit(qk_ref, wconv_ref, c, r0):
    cs = slice(c * LANES, (c + 1) * LANES)
    w = wconv_ref[:, cs]
    y = None
    for tap in range(CONV_WIDTH):
        lo = r0 + HALO - CONV_WIDTH // 2 + tap
        term = qk_ref[lo:lo + CHUNK, cs] * w[tap:tap + 1, :]
        y = term if y is None else y + term
    return jax.nn.silu(y)


def _gate_scan_start(gt, tri):
    nh = N_MLSTM_HEADS
    out = []
    for d in range(2):
        ig = gt[2 * d * nh:(2 * d + 1) * nh, :]
        fg = gt[(2 * d + 1) * nh:(2 * d + 2) * nh, :]
        lf = jnp.minimum(fg, 0.0) - jnp.log1p(jnp.exp(-jnp.abs(fg)))
        parts = jnp.concatenate(_split3(lf), axis=0).astype(_BF16)
        sums = jnp.dot(parts, tri[d], preferred_element_type=_F32)
        out.append((ig, sums[0:nh] + sums[nh:2 * nh] + sums[2 * nh:3 * nh]))
    return out


def _gate_scan_finish(scans, st_ref, dec_ref, cs):
    nh = N_MLSTM_HEADS
    lane8 = lax.broadcasted_iota(jnp.int32, (nh, CHUNK), 1)
    ones = jnp.ones((3 * nh, CHUNK), _F32)
    zpad = jnp.zeros((DEC_K - 6 * nh, CHUNK), _F32)
    dec_rows = []
    for d, (ig, a) in enumerate(scans):
        r = ig - a
        gl = r
        sh = 1
        while sh < CHUNK:
            if d == 0:
                gl = jnp.maximum(gl, jnp.where(lane8 >= sh, pltpu.roll(gl, sh, 1), NEG))
            else:
                gl = jnp.maximum(
                    gl, jnp.where(lane8 < CHUNK - sh, pltpu.roll(gl, CHUNK - sh, 1), NEG))
            sh *= 2
        far = CHUNK - 1 if d == 0 else 0
        a_end = jnp.broadcast_to(a[:, far:far + 1], (nh, CHUNK))
        m_loc = a_end + jnp.broadcast_to(gl[:, far:far + 1], (nh, CHUNK))
        e = jnp.exp(a_end + r - m_loc)
        st_ref[d * ST_ROWS:(d + 1) * ST_ROWS, cs] = jnp.concatenate(
            [a, gl, e, a_end, m_loc], axis=0)
        dec_rows += list(_split3(r * LOG2E)) + [ones, zpad]
    dec_ref[cs, :] = jnp.concatenate(dec_rows, axis=0).T.astype(_BF16)


def _ffn_proj_kernel(xm_ref, xn_ref, g1_ref, wgu_ref, wd_ref, gmix_ref, wqk_ref, wn_ref,
                     wt_ref, bgate_ref, wconv_ref,
                     h_ref, pn_ref, pt_ref, k_ref, qt_ref, st_ref, dec_ref, s_ref,
                     qk_scr, gt_scr, *, tiles_per_seq):
    j = pl.program_id(0)
    tm = xm_ref.shape[0]
    n_chunks = tm // CHUNK

    @pl.when(j == 0)
    def _():
        qk_scr[...] = jnp.zeros_like(qk_scr)
        gt_scr[...] = jnp.zeros_like(gt_scr)

    srow = lax.broadcasted_iota(jnp.int32, (CHUNK, CHUNK), 0)
    slane = lax.broadcasted_iota(jnp.int32, (CHUNK, CHUNK), 1)
    tri = [(srow <= slane).astype(_BF16), (srow >= slane).astype(_BF16)]
    scans = [_gate_scan_start(gt_scr[:, c * CHUNK:(c + 1) * CHUNK], tri)
             for c in range(n_chunks)]

    n_pairs = N_MLSTM_HEADS // 2
    zero_head = jnp.zeros((HEAD_DIM, CHUNK), _BF16)

    def conv_unit(p, c):
        r0 = c * CHUNK
        qt = _conv_unit(qk_scr, wconv_ref, p, r0).T.astype(_BF16)
        k = (_conv_unit(qk_scr, wconv_ref, n_pairs + p, r0) * (HEAD_DIM ** -0.5)).astype(_BF16)
        qt_ref[p * LANES:(p + 1) * LANES, r0:r0 + CHUNK] = qt
        k_ref[r0:r0 + CHUNK, p * LANES:(p + 1) * LANES] = k
        return p, c, k, qt

    def score_unit(p, c, k, qt):
        for par in range(2):
            half = qt[par * HEAD_DIM:(par + 1) * HEAD_DIM]
            qtz = jnp.concatenate([zero_head, half] if par else [half, zero_head], axis=0)
            s_ref[c, 2 * p + par] = jnp.dot(k, qtz, preferred_element_type=_F32).astype(_BF16)

    conv_units = [functools.partial(conv_unit, p, c)
                  for c in range(n_chunks) for p in range(n_pairs)]
    scan_units = [functools.partial(_gate_scan_finish, scans[c], st_ref, dec_ref,
                                    slice(c * CHUNK, (c + 1) * CHUNK)) for c in range(n_chunks)]

    pos = jnp.minimum(j, pl.num_programs(0) - 2) % tiles_per_seq
    x = jnp.concatenate([xm_ref[...], xn_ref[...]], axis=0)
    xn = _rmsnorm(x, g1_ref[...]).astype(_BF16)
    per_slice = -(-len(conv_units) // (len(FF_SLICES) - 1))
    convolved = []
    acts = []
    for lo, hi in FF_SLICES:
        for done in convolved:
            score_unit(*done)
        g = jnp.dot(xn, wgu_ref[:, lo:hi], preferred_element_type=_F32)
        u = jnp.dot(xn, wgu_ref[:, D_FF + lo:D_FF + hi], preferred_element_type=_F32)
        convolved = [unit() for unit in conv_units[:per_slice]]
        del conv_units[:per_slice]
        if not convolved and scan_units:
            scan_units.pop(0)()
        acts.append((jax.nn.silu(g) * u).astype(_BF16))
    for done in convolved:
        score_unit(*done)
    acc = jnp.dot(jnp.concatenate(acts, axis=1), wd_ref[...], preferred_element_type=_F32)
    while scan_units:
        scan_units.pop(0)()
    h = x + 0.5 * acc
    h_ref[...] = h[0:tm]
    un = _rmsnorm(h, gmix_ref[...]).astype(_BF16)
    qk = jnp.dot(un, wqk_ref[...], preferred_element_type=_F32)
    prev_tail = qk_scr[tm:tm + HALO, :]
    qk_scr[0:HALO, :] = jnp.where(pos > 0, prev_tail, 0.0)
    qk_scr[HALO:HALO + tm, :] = qk[0:tm]
    qk_scr[HALO + tm:, :] = jnp.where(pos < tiles_per_seq - 1, qk[tm:], 0.0)
    un_c = un[0:tm]
    pn_ref[...] = jnp.dot(un_c, wn_ref[...], preferred_element_type=_F32).astype(_BF16)
    t = lax.dot_general(wt_ref[...], un_c, (((1,), (1,)), ((), ())),
                        preferred_element_type=_F32)
    pt_ref[...] = t[:PT_ROWS].astype(_BF16)
    gt_scr[...] = t[PT_ROWS:PT_ROWS + N_GATES] + bgate_ref[...]


def _ffn_proj(x2, seq, g1, wgu, wd, gmix, wqk, wn, wt, bgate, wconv):
    n = x2.shape[0]
    tm = TOKEN_TILE
    nt = n // tm
    per_seq = seq // tm
    per = tm // HALO
    nhalo = n // HALO
    ta = lambda j: jnp.minimum(j, nt - 1)
    tf = lambda j: jnp.maximum(j - 1, 0)
    rowa = lambda w: pl.BlockSpec((tm, w), lambda j: (ta(j), 0))
    rowf = lambda w: pl.BlockSpec((tm, w), lambda j: (tf(j), 0))
    cola = lambda r: pl.BlockSpec((None, r, tm), lambda j: (ta(j) // per_seq, 0, ta(j) % per_seq))
    colf = lambda r: pl.BlockSpec((None, r, tm), lambda j: (tf(j) // per_seq, 0, tf(j) % per_seq))
    b = n // seq
    return pl.pallas_call(
        functools.partial(_ffn_proj_kernel, tiles_per_seq=per_seq),
        grid=(nt + 1,),
        in_specs=[rowa(D_MODEL),
                  pl.BlockSpec((HALO, D_MODEL),
                               lambda j: (jnp.minimum((ta(j) + 1) * per, nhalo - 1), 0)),
                  _const_spec((1, D_MODEL)), _const_spec((D_MODEL, 2 * D_FF)),
                  _const_spec((D_FF, D_MODEL)), _const_spec((1, D_MODEL)),
                  _const_spec((D_MODEL, 2 * MLSTM_WIDTH)),
                  _const_spec((D_MODEL, PN_WIDTH)),
                  _const_spec((PT_ROWS + GATE_ROWS_PAD, D_MODEL)),
                  _const_spec((N_GATES, 1)), _const_spec((SUBLANES, 2 * MLSTM_WIDTH))],
        out_specs=[rowa(D_MODEL), rowa(PN_WIDTH), cola(PT_ROWS),
                   rowf(MLSTM_WIDTH), colf(MLSTM_WIDTH), colf(2 * ST_ROWS), rowf(LANES),
                   pl.BlockSpec((tm // CHUNK, N_MLSTM_HEADS, CHUNK, CHUNK),
                                lambda j: (tf(j), 0, 0, 0))],
        out_shape=[jax.ShapeDtypeStruct((n, D_MODEL), _F32),
                   jax.ShapeDtypeStruct((n, PN_WIDTH), _BF16),
                   jax.ShapeDtypeStruct((b, PT_ROWS, seq), _BF16),
                   jax.ShapeDtypeStruct((n, MLSTM_WIDTH), _BF16),
                   jax.ShapeDtypeStruct((b, MLSTM_WIDTH, seq), _BF16),
                   jax.ShapeDtypeStruct((b, 2 * ST_ROWS, seq), _F32),
                   jax.ShapeDtypeStruct((n, LANES), _BF16),
                   jax.ShapeDtypeStruct((n // CHUNK, N_MLSTM_HEADS, CHUNK, CHUNK), _BF16)],
        scratch_shapes=[pltpu.VMEM((tm + 2 * HALO, 2 * MLSTM_WIDTH), _F32),
                        pltpu.VMEM((N_GATES, tm), _F32)],
        compiler_params=pltpu.CompilerParams(dimension_semantics=("arbitrary",),
                                             vmem_limit_bytes=VMEM_LIMIT),
        name="ffn1_proj_prep",
    )(x2, x2, g1, wgu, wd, gmix, wqk, wn, wt, bgate, wconv)


def _attn_scores(qt_ref, kcat, blk, h):
    zeros = jnp.zeros((HEAD_DIM, BLOCK), _BF16)
    keys = kcat[blk * BLOCK:(blk + 3) * BLOCK, :]
    qt = (qt_ref[h * HEAD_DIM:(h + 1) * HEAD_DIM, blk * BLOCK:(blk + 1) * BLOCK]
          * (HEAD_DIM ** -0.5))
    qtz = jnp.concatenate([qt, zeros] if h < GQA_GROUP else [zeros, qt], axis=0)
    return jnp.dot(keys, qtz, preferred_element_type=_F32)


def _attn_softmax(s, sink_ref, bias_ref, h, edge_ok):
    s = s + bias_ref[h]
    if edge_ok is not None:
        s = jnp.where(edge_ok, s, NEG)
    sk = sink_ref[h]
    m = jnp.maximum(jnp.max(s, axis=0, keepdims=True), sk)
    p = jnp.exp(s - m)
    den = jnp.sum(p, axis=0, keepdims=True) + jnp.exp(sk - m)
    return p.astype(_BF16), 1.0 / den


def _attn_values(p, rden, vcat, blk, h):
    g = h // GQA_GROUP
    vtg = vcat[g * HEAD_DIM:(g + 1) * HEAD_DIM, blk * BLOCK:(blk + 3) * BLOCK]
    return jnp.dot(vtg, p, preferred_element_type=_F32) * rden


def _mix_ffn_kernel(sink_ref, qt_ref, kp_ref, kc_ref, kn_ref, vp_ref, vc_ref, vn_ref, bias_ref,
                    h_ref, hf_ref, hb_ref, o_ref, gout_ref, wout_ref, g2_ref, wgu_ref, wd_ref,
                    gf_ref, y_ref, attn_ref, *, tiles_per_seq):
    j = pl.program_id(0)
    tm = h_ref.shape[0]
    nblk = tm // BLOCK

    @pl.when(j == 0)
    def _():
        attn_ref[...] = jnp.zeros_like(attn_ref)

    pos = jnp.minimum(j, pl.num_programs(0) - 2) % tiles_per_seq
    kcat = jnp.concatenate([kp_ref[...], kc_ref[...], kn_ref[...]], axis=0)
    vcat = jnp.concatenate([vp_ref[...], vc_ref[...], vn_ref[...]], axis=1)
    krow = lax.broadcasted_iota(jnp.int32, (3 * BLOCK, BLOCK), 0)
    edge = {0: (krow >= BLOCK) | (pos > 0)}
    ok_next = (krow < 2 * BLOCK) | (pos < tiles_per_seq - 1)
    edge[nblk - 1] = ok_next if nblk > 1 else edge[0] & ok_next
    units = [(blk, h) for blk in range(nblk) for h in range(N_ATTN_HEADS)]
    n_slices = len(FF_SLICES)
    per = -(-len(units) // n_slices)
    groups = [units[i * per:(i + 1) * per] for i in range(n_slices)]
    attn_out = {}
    scored = [_attn_scores(qt_ref, kcat, blk, hh) for blk, hh in groups[0]]
    weighted = []

    lane = lax.broadcasted_iota(jnp.int32, (tm, LANES), 1)
    lo_half = lane < HEAD_DIM
    hm_tiles = []
    for p in range(N_MLSTM_HEADS // 2):
        sl = slice(p * LANES, (p + 1) * LANES)
        hm = (jax.nn.sigmoid(o_ref[:, sl].astype(_F32))
              * (hf_ref[:, sl].astype(_F32) + hb_ref[:, sl].astype(_F32)))
        sq = hm * hm
        ss_lo = jnp.sum(jnp.where(lo_half, sq, 0.0), axis=1, keepdims=True)
        ss_hi = jnp.sum(jnp.where(lo_half, 0.0, sq), axis=1, keepdims=True)
        ms = jnp.where(lo_half, ss_lo, ss_hi) * (1.0 / HEAD_DIM)
        hm_tiles.append((hm * lax.rsqrt(ms + EPS) * gout_ref[:, sl]).astype(_BF16))
    hm_all = jnp.concatenate(hm_tiles, axis=1)
    h = (h_ref[...]
         + jnp.dot(attn_ref[...], wout_ref[0:ATTN_WIDTH, :], preferred_element_type=_F32)
         + jnp.dot(hm_all, wout_ref[ATTN_WIDTH:MIX_WIDTH, :], preferred_element_type=_F32))
    hn = _rmsnorm(h, g2_ref[...]).astype(_BF16)
    acts = []
    for t, (lo, hi) in enumerate(FF_SLICES):
        upcoming = ([_attn_scores(qt_ref, kcat, blk, hh) for blk, hh in groups[t + 1]]
                    if t + 1 < n_slices else [])
        g = jnp.dot(hn, wgu_ref[:, lo:hi], preferred_element_type=_F32)
        u = jnp.dot(hn, wgu_ref[:, D_FF + lo:D_FF + hi], preferred_element_type=_F32)
        for (blk, hh), p, rden in weighted:
            attn_out[blk, hh] = _attn_values(p, rden, vcat, blk, hh)
        weighted = [((blk, hh),) + _attn_softmax(s, sink_ref, bias_ref, hh, edge.get(blk))
                    for (blk, hh), s in zip(groups[t], scored)]
        acts.append((jax.nn.silu(g) * u).astype(_BF16))
        scored = upcoming
    for (blk, hh), p, rden in weighted:
        attn_out[blk, hh] = _attn_values(p, rden, vcat, blk, hh)
    acc = jnp.dot(jnp.concatenate(acts, axis=1), wd_ref[...], preferred_element_type=_F32)
    h = h + 0.5 * acc
    y_ref[...] = _rmsnorm(h, gf_ref[...])

    for blk in range(nblk):
        heads = [attn_out[blk, hh] for hh in range(N_ATTN_HEADS)]
        attn_ref[blk * BLOCK:(blk + 1) * BLOCK, :] = (
            jnp.concatenate(heads, axis=0).T.astype(_BF16))


def _mix_ffn(seq, pn2, pt3, h2, hf2, hb2, sink, bias_t, gout, wout, g2, wgu, wd, gf):
    n = h2.shape[0]
    tm = TOKEN_TILE
    nt = n // tm
    per_seq = seq // tm
    per_blk = tm // BLOCK
    nblocks = n // BLOCK
    blk_per_seq = seq // BLOCK
    ta = lambda j: jnp.minimum(j, nt - 1)
    tf = lambda j: jnp.maximum(j - 1, 0)
    kcol = PN_K_A // KV_WIDTH
    vrow = PT_V_A // KV_WIDTH
    kprev = lambda j: jnp.maximum(ta(j) * per_blk - 1, 0)
    knext = lambda j: jnp.minimum((ta(j) + 1) * per_blk, nblocks - 1)
    vmain = lambda j: (ta(j) // per_seq, vrow, ta(j) % per_seq)
    vprev = lambda j: (kprev(j) // blk_per_seq, vrow, kprev(j) % blk_per_seq)
    vnext = lambda j: (knext(j) // blk_per_seq, vrow, knext(j) % blk_per_seq)
    row = lambda w, c=0: pl.BlockSpec((tm, w), lambda j: (tf(j), c))
    return pl.pallas_call(
        functools.partial(_mix_ffn_kernel, tiles_per_seq=per_seq),
        grid=(nt + 1,),
        in_specs=[pl.BlockSpec(memory_space=pltpu.SMEM),
                  pl.BlockSpec((None, ATTN_WIDTH, tm),
                               lambda j: (ta(j) // per_seq, PT_Q_A // ATTN_WIDTH, ta(j) % per_seq)),
                  pl.BlockSpec((BLOCK, KV_WIDTH), lambda j: (kprev(j), kcol)),
                  pl.BlockSpec((tm, KV_WIDTH), lambda j: (ta(j), kcol)),
                  pl.BlockSpec((BLOCK, KV_WIDTH), lambda j: (knext(j), kcol)),
                  pl.BlockSpec((None, KV_WIDTH, BLOCK), vprev),
                  pl.BlockSpec((None, KV_WIDTH, tm), vmain),
                  pl.BlockSpec((None, KV_WIDTH, BLOCK), vnext),
                  _const_spec((N_ATTN_HEADS, 3 * BLOCK, BLOCK)),
                  row(D_MODEL), row(MLSTM_WIDTH), row(MLSTM_WIDTH),
                  row(MLSTM_WIDTH, PN_O_M // MLSTM_WIDTH),
                  _const_spec((1, MLSTM_WIDTH)), _const_spec((MIX_WIDTH, D_MODEL)),
                  _const_spec((1, D_MODEL)), _const_spec((D_MODEL, 2 * D_FF)),
                  _const_spec((D_FF, D_MODEL)), _const_spec((1, D_MODEL))],
        out_specs=row(D_MODEL),
        out_shape=jax.ShapeDtypeStruct((n, D_MODEL), _F32),
        scratch_shapes=[pltpu.VMEM((tm, ATTN_WIDTH), _BF16)],
        compiler_params=pltpu.CompilerParams(dimension_semantics=("arbitrary",),
                                             vmem_limit_bytes=VMEM_LIMIT),
        name="attn_out_ffn2",
    )(sink, pt3, pn2, pn2, pn2, pt3, pt3, pt3, bias_t, h2, hf2, hb2, pn2,
      gout, wout, g2, wgu, wd, gf)


def _chunk_rows(d, st, m_prev):
    nh = N_MLSTM_HEADS
    a = st[ST_A:ST_A + nh]
    gl = st[ST_GL:ST_GL + nh]
    a_end = st[ST_AEND:ST_AEND + nh]
    m_loc = st[ST_MLOC:ST_MLOC + nh]
    g = jnp.maximum(m_prev, gl)
    m_new = jnp.maximum(a_end + m_prev, m_loc)
    row8 = lax.broadcasted_iota(jnp.int32, (nh, CHUNK), 0)
    zero8 = jnp.zeros((nh, CHUNK), _F32)
    zero_half = jnp.zeros((DEC_K, CHUNK), _F32)
    decay_rhs = []
    for h in range(nh):
        hot = row8 == h
        one = jnp.where(hot, 1.0, 0.0)
        blk = jnp.concatenate([one, one, one] + [jnp.where(hot, t, 0.0) for t in _split3(-g * LOG2E)]
                              + [zero8, zero8], axis=0)
        x = jnp.concatenate([zero_half, blk] if d else [blk, zero_half], axis=0)
        decay_rhs.append(x.astype(_BF16))
    return dict(e=st[ST_E:ST_E + nh], decay_rhs=decay_rhs,
                inter=jnp.exp(m_prev - g),
                floor=jnp.exp(-(a + g)),
                m_new=m_new,
                s_old=jnp.exp(a_end + m_prev - m_new), s_loc=jnp.exp(m_loc - m_new))


def _chunk_input_matmuls(d, rows, k, qt, vt, dec):
    nh = N_MLSTM_HEADS
    zero_head = jnp.zeros((HEAD_DIM, CHUNK), _BF16)
    e = rows["e"]
    heads = []
    for h in range(nh):
        p, par = divmod(h, 2)
        hs = slice(h * HEAD_DIM, (h + 1) * HEAD_DIM)
        qt_h = qt[hs, :]
        qtz = jnp.concatenate([zero_head, qt_h] if par else [qt_h, zero_head], axis=0)
        k_pair = k[:, p * LANES:(p + 1) * LANES]
        vt_h = vt[hs, :]
        heads.append(dict(
            qtz=qtz, vt=vt_h,
            dexp=jnp.dot(dec, rows["decay_rhs"][h], preferred_element_type=_F32),
            c_loc=jnp.dot((vt_h.astype(_F32) * e[h:h + 1, :]).astype(_BF16), k_pair,
                          preferred_element_type=_F32)))
    n_loc = jnp.dot(e.astype(_BF16), k, preferred_element_type=_F32)
    return heads, n_loc


def _chunk_state_matmuls(heads, qt, c_state, n_state):
    for hd, cz in zip(heads, c_state):
        hd["nd_state"] = jnp.dot(cz.astype(_BF16), hd["qtz"], preferred_element_type=_F32)
    return jnp.dot(n_state.astype(_BF16), qt, preferred_element_type=_F32)


def _state_update(rows, heads, n_loc, c_state, n_state):
    nh = N_MLSTM_HEADS
    s_old, s_loc = rows["s_old"], rows["s_loc"]
    c_new = [s_old[h:h + 1, :] * cz + s_loc[h:h + 1, :] * hd["c_loc"]
             for h, (hd, cz) in enumerate(zip(heads, c_state))]
    rep = MLSTM_WIDTH // LANES
    head_of_lane = lax.broadcasted_iota(jnp.int32, (nh, MLSTM_WIDTH), 1) // HEAD_DIM
    own = head_of_lane == lax.broadcasted_iota(jnp.int32, (nh, MLSTM_WIDTH), 0)
    n_new = (jnp.concatenate([s_old] * rep, axis=1) * n_state
             + jnp.concatenate([s_loc] * rep, axis=1) * jnp.where(own, n_loc, 0.0))
    return c_new, n_new


def _chunk_weighted_values(d, heads, s_ref, i):
    row = lax.broadcasted_iota(jnp.int32, (CHUNK, CHUNK), 0)
    lane = lax.broadcasted_iota(jnp.int32, (CHUNK, CHUNK), 1)
    keep = (row >= lane) if d else (row <= lane)
    for h, hd in enumerate(heads):
        qk = s_ref[i, h].astype(_F32) * jnp.exp2(jnp.where(keep, hd["dexp"], NEG))
        hd["den_intra"] = jnp.sum(qk, axis=0, keepdims=True)
        hd["nd_intra"] = jnp.dot(hd["vt"], qk.astype(_BF16), preferred_element_type=_F32)


def _chunk_finish(rows, heads, den_state, out_ref, rs):
    inter, floor = rows["inter"], rows["floor"]
    for p in range(N_MLSTM_HEADS // 2):
        tiles = []
        for h in (2 * p, 2 * p + 1):
            hd = heads[h]
            it = inter[h:h + 1, :]
            den = hd["den_intra"] + it * den_state[h:h + 1, :]
            num = hd["nd_intra"] + it * hd["nd_state"]
            tiles.append(num / jnp.maximum(jnp.abs(den), floor[h:h + 1, :]))
        out_ref[rs, p * LANES:(p + 1) * LANES] = jnp.concatenate(tiles, axis=0).T.astype(_BF16)


def _mlstm_seq_kernel(kf_ref, qtf_ref, vtf_ref, decf_ref, stf_ref, sf_ref,
                      kb_ref, qtb_ref, vtb_ref, decb_ref, stb_ref, sb_ref,
                      hf_ref, hb_ref, c_ref, n_ref, m_ref):
    @pl.when(pl.program_id(1) == 0)
    def _():
        c_ref[...] = jnp.zeros_like(c_ref)
        n_ref[...] = jnp.zeros_like(n_ref)
        m_ref[...] = jnp.full_like(m_ref, NEG)

    nh = N_MLSTM_HEADS
    n_sub = kf_ref.shape[0] // CHUNK
    refs = [(kf_ref, qtf_ref, vtf_ref, decf_ref, stf_ref, hf_ref),
            (kb_ref, qtb_ref, vtb_ref, decb_ref, stb_ref, hb_ref)]
    score_refs = [sf_ref, sb_ref]
    order = [list(range(n_sub)), list(range(n_sub - 1, -1, -1))]
    c_state = [[c_ref[d, h] for h in range(nh)] for d in range(2)]
    n_state = [n_ref[d] for d in range(2)]
    rows = [[], []]
    for d in range(2):
        m = m_ref[d]
        for i in order[d]:
            rows[d].append(_chunk_rows(d, refs[d][4][:, i * CHUNK:(i + 1) * CHUNK], m))
            m = rows[d][-1]["m_new"]
        m_ref[d] = m

    work = [[None] * n_sub, [None] * n_sub]

    def start(step):
        for d in range(2):
            k_ref, qt_ref, vt_ref, dec_ref, _, _ = refs[d]
            ts = slice(order[d][step] * CHUNK, (order[d][step] + 1) * CHUNK)
            heads, n_loc = _chunk_input_matmuls(d, rows[d][step], k_ref[ts, :], qt_ref[:, ts],
                                                vt_ref[:, ts], dec_ref[ts, :])
            work[d][step] = dict(heads=heads, n_loc=n_loc, ts=ts)
        for d in range(2):
            w = work[d][step]
            w["den_state"] = _chunk_state_matmuls(w["heads"], refs[d][1][:, w["ts"]],
                                                  c_state[d], n_state[d])
        for d in range(2):
            w = work[d][step]
            c_state[d], n_state[d] = _state_update(rows[d][step], w["heads"], w["n_loc"],
                                                   c_state[d], n_state[d])

    def weigh(step):
        for d in range(2):
            _chunk_weighted_values(d, work[d][step]["heads"], score_refs[d], order[d][step])

    def finish(step):
        for d in range(2):
            w = work[d][step]
            _chunk_finish(rows[d][step], w["heads"], w["den_state"], refs[d][5], w["ts"])

    for step in range(n_sub + 2):
        if step < n_sub:
            start(step)
        if 1 <= step <= n_sub:
            weigh(step - 1)
        if step >= 2:
            finish(step - 2)
    for d in range(2):
        for h in range(nh):
            c_ref[d, h] = c_state[d][h]
        n_ref[d] = n_state[d]


def _mlstm_seq(k3, qt3, pt3, dec3, st3, sc5):
    b, s, _ = k3.shape
    sub = SEQ_CHUNKS if (s // CHUNK) % SEQ_CHUNKS == 0 else 1
    blk = sub * CHUNK
    nc = s // blk
    vrow = PT_V_M // MLSTM_WIDTH

    def specs(pos, d):
        return [pl.BlockSpec((None, blk, MLSTM_WIDTH), lambda bi, c: (bi, pos(c), 0)),
                pl.BlockSpec((None, MLSTM_WIDTH, blk), lambda bi, c: (bi, 0, pos(c))),
                pl.BlockSpec((None, MLSTM_WIDTH, blk), lambda bi, c: (bi, vrow, pos(c))),
                pl.BlockSpec((None, blk, LANES), lambda bi, c: (bi, pos(c), 0)),
                pl.BlockSpec((None, ST_ROWS, blk), lambda bi, c: (bi, d, pos(c))),
                pl.BlockSpec((None, sub, N_MLSTM_HEADS, CHUNK, CHUNK),
                             lambda bi, c: (bi, pos(c), 0, 0, 0))]

    fwd = lambda c: c
    bwd = lambda c: nc - 1 - c
    out = lambda pos: pl.BlockSpec((None, blk, MLSTM_WIDTH), lambda bi, c: (bi, pos(c), 0))
    args = (k3, qt3, pt3, dec3, st3, sc5)
    return pl.pallas_call(
        _mlstm_seq_kernel,
        grid=(b, nc),
        in_specs=specs(fwd, 0) + specs(bwd, 1),
        out_specs=[out(fwd), out(bwd)],
        out_shape=[jax.ShapeDtypeStruct((b, s, MLSTM_WIDTH), _BF16)] * 2,
        scratch_shapes=[pltpu.VMEM((2, N_MLSTM_HEADS, HEAD_DIM, LANES), _F32),
                        pltpu.VMEM((2, N_MLSTM_HEADS, MLSTM_WIDTH), _F32),
                        pltpu.VMEM((2, N_MLSTM_HEADS, LANES), _F32)],
        compiler_params=pltpu.CompilerParams(dimension_semantics=("parallel", "arbitrary")),
        name="mlstm_seq",
    )(*args, *args)


def _t5_bucket(rel):
    nb = N_BUCKETS // 2
    ret = (rel > 0).astype(np.int32) * nb
    n = np.abs(rel)
    max_exact = nb // 2
    large = max_exact + (np.log(np.maximum(n, 1) / max_exact)
                         / math.log(MAX_DISTANCE / max_exact) * (nb - max_exact)).astype(np.int32)
    large = np.minimum(large, nb - 1)
    return (ret + np.where(n < max_exact, n, large)).astype(np.int32)


def _prepare_params(g_ffn1, w_ffn1_gu, w_ffn1_down, g_mix, w_in, w_conv, b_gates, attn_sink,
                    g_mlstm_out, w_out, g_ffn2, w_ffn2_gu, w_ffn2_down, rel_table, g_final):
    row = lambda g: g.reshape(1, -1).astype(_F32)
    offs = np.cumsum((0,) + SPLIT_SIZES)
    q_a, k_a, v_a, q_m, k_m, v_m, o_m, gate = (w_in[0][:, offs[i]:offs[i + 1]] for i in range(8))
    wqk = jnp.concatenate([q_m, k_m], axis=1).astype(_BF16)
    wn = jnp.concatenate([o_m, k_a], axis=1).astype(_BF16)
    gate_pad = jnp.pad(gate, ((0, 0), (0, GATE_ROWS_PAD - N_GATES)))
    wt = jnp.concatenate([q_a, v_m, v_a, gate_pad], axis=1).T.astype(_BF16)
    bgate = b_gates[0].reshape(N_GATES, 1).astype(_F32)
    wconv = jnp.pad(w_conv[0].astype(_F32), ((0, SUBLANES - CONV_WIDTH), (0, 0)))
    kj = np.arange(3 * BLOCK)[:, None]
    qi = np.arange(BLOCK)[None, :]
    bucket = jnp.asarray(_t5_bucket((kj - BLOCK) - qi).reshape(-1))
    onehot = (bucket[None, :] == jnp.arange(N_BUCKETS)[:, None]).astype(_F32)
    bias_t = jnp.dot(rel_table.astype(_F32).T, onehot, precision=lax.Precision.HIGHEST)
    bias_t = bias_t.reshape(N_ATTN_HEADS, 3 * BLOCK, BLOCK)
    window = np.abs((kj - BLOCK) - qi) <= WINDOW
    bias_t = jnp.where(jnp.asarray(window)[None], bias_t, NEG)
    return dict(
        g1=row(g_ffn1[0]), wgu1=w_ffn1_gu[0].astype(_BF16), wd1=w_ffn1_down[0].astype(_BF16),
        gmix=row(g_mix[0]), wqk=wqk, wn=wn, wt=wt, bgate=bgate, wconv=wconv,
        sink=attn_sink[0].astype(_F32), bias_t=bias_t, gout=row(g_mlstm_out[0]),
        wout=w_out[0].astype(_BF16), g2=row(g_ffn2[0]), wgu2=w_ffn2_gu[0].astype(_BF16),
        wd2=w_ffn2_down[0].astype(_BF16), gf=row(g_final))


def _trunk(x, p):
    b, s, d = x.shape
    n = b * s
    h2, pn2, pt3, k2, qt3, st3, dec2, sc4 = _ffn_proj(
        x.reshape(n, d), s, p["g1"], p["wgu1"], p["wd1"], p["gmix"], p["wqk"], p["wn"], p["wt"],
        p["bgate"], p["wconv"])
    hf3, hb3 = _mlstm_seq(k2.reshape(b, s, MLSTM_WIDTH), qt3, pt3, dec2.reshape(b, s, LANES), st3,
                          sc4.reshape(b, s // CHUNK, N_MLSTM_HEADS, CHUNK, CHUNK))
    y2 = _mix_ffn(s, pn2, pt3, h2, hf3.reshape(n, MLSTM_WIDTH), hb3.reshape(n, MLSTM_WIDTH),
                  p["sink"], p["bias_t"], p["gout"], p["wout"], p["g2"], p["wgu2"], p["wd2"],
                  p["gf"])
    return y2.reshape(b, s, d)


def kernel(x_prompt, x_sample, g_ffn1, w_ffn1_gu, w_ffn1_down, g_mix, w_in, w_conv, b_gates,
           attn_sink, g_mlstm_out, w_out, g_ffn2, w_ffn2_gu, w_ffn2_down, rel_bias_table, g_final):
    p = _prepare_params(g_ffn1, w_ffn1_gu, w_ffn1_down, g_mix, w_in, w_conv, b_gates, attn_sink,
                        g_mlstm_out, w_out, g_ffn2, w_ffn2_gu, w_ffn2_down, rel_bias_table, g_final)
    return (_trunk(x_prompt, p), _trunk(x_sample, p))
```
